```python
import jax
import jax.numpy as jnp
from jax import lax
import numpy as np

D_MODEL = 1024
BATCH = 4
SEQ = 4096
DEPTH = 4

CTX_LEN = 256
GRID_W = 64
D_MIX = 1024

RG_WIDTH = 384
RG_BLOCKS = 6
RG_BLOCK = RG_WIDTH // RG_BLOCKS
RG_C = 8.0
CONV_K = 4

NA_HEADS = 6
NA_HEAD_DIM = 64
NA_WIDTH = NA_HEADS * NA_HEAD_DIM
NA_ROWS = 8
NA_COLS = 16

GDN_HEADS = 4
GDN_DK = 64
GDN_DV = 64
GDN_QK = GDN_HEADS * GDN_DK
GDN_WIDTH = GDN_HEADS * GDN_DV
GDN_CHUNK = 64
ROPE_BASE = 10000.0

CONV_CH = RG_WIDTH + 2 * GDN_QK + GDN_WIDTH
CONV_SPLIT = (RG_WIDTH, RG_WIDTH + GDN_QK, RG_WIDTH + 2 * GDN_QK)
REST_SIZES = (RG_WIDTH, NA_WIDTH, NA_WIDTH, NA_WIDTH, NA_WIDTH, GDN_WIDTH, 2 * GDN_HEADS, 2 * GDN_HEADS)
REST_SPLIT = tuple(int(v) for v in np.cumsum(REST_SIZES)[:-1])
D_IN = CONV_CH + sum(REST_SIZES)

DEEPNORM_ALPHA = (2.0 * DEPTH) ** 0.25
DEEPNORM_BETA = (8.0 * DEPTH) ** -0.25
LN_EPS = 1e-5
NORM_EPS = 1e-6
F32 = jnp.float32

kernel_name = 'hybrid_rglru_natten_gdn_prefix_dit'


def layer_norm(x, g, b):
    xf = x.astype(F32)
    mu = jnp.mean(xf, -1, keepdims=True)
    var = jnp.mean(jnp.square(xf - mu), -1, keepdims=True)
    return ((xf - mu) * lax.rsqrt(var + LN_EPS)).astype(x.dtype) * g + b


def split_heads(t, n_heads):
    return t.reshape(t.shape[:-1] + (n_heads, t.shape[-1] // n_heads))


def flip_seq(t, d):
    return t[:, ::-1] if d else t


def depthwise_conv_centred(u, w):
    k = w.shape[0]
    return lax.conv_general_dilated(u, w[:, None, :].astype(u.dtype), window_strides=(1,),
                                    padding=[(k // 2, k - 1 - k // 2)],
                                    dimension_numbers=('NWC', 'WIO', 'NWC'),
                                    feature_group_count=u.shape[-1])


def combined_projection(u, w_in, conv_w):
    p = u @ w_in
    conv = depthwise_conv_centred(p[..., :CONV_CH], conv_w)
    xa, qg, kg, vg = jnp.split(conv, CONV_SPLIT, axis=-1)
    za, qn, kn, vn, zn, zg, b_raw, a_raw = jnp.split(p[..., CONV_CH:], REST_SPLIT, axis=-1)
    return (xa, qg, kg, vg, za, qn, kn, vn, zn, zg, b_raw, a_raw)


def _linear_combine(left, right):
    a1, b1 = left
    a2, b2 = right
    return a1 * a2, a2 * b1 + b2


def block_diag_linear(u, w, b):
    ub = u.reshape(u.shape[:-1] + (RG_BLOCKS, RG_BLOCK))
    return jnp.einsum('blnd,nde->blne', ub, w).reshape(u.shape) + b


def rglru_scan(u, w_a, b_a, w_x, b_x, lam, h0):
    r = jax.nn.sigmoid(block_diag_linear(u, w_a, b_a))
    i = jax.nn.sigmoid(block_diag_linear(u, w_x, b_x))
    log_a = RG_C * r * jax.nn.log_sigmoid(lam)
    a = jnp.exp(log_a)
    b = jnp.sqrt(-jnp.expm1(2.0 * log_a)) * (i * u)
    b = b.at[:, 0].add(a[:, 0] * h0)
    _, h = lax.associative_scan(_linear_combine, (a, b), axis=1)
    return h


def rglru_bidir(u_lat, u_ctx, w_a, b_a, w_x, b_x, lam, ctx_out):
    dt = u_lat.dtype
    ul, uc = u_lat.astype(F32), u_ctx.astype(F32)
    h0 = jnp.zeros(uc.shape[:1] + uc.shape[2:], F32)
    lat, ctx = [], []
    for d in range(2):
        prm = (w_a[d], b_a[d], w_x[d], b_x[d], lam[d].astype(F32))
        hc = rglru_scan(flip_seq(uc, d), *prm, h0)
        hl = rglru_scan(flip_seq(ul, d), *prm, hc[:, -1])
        lat.append(flip_seq(hl, d))
        ctx.append(flip_seq(hc, d))
    y_ctx = (ctx[0] + ctx[1]).astype(dt) if ctx_out else None
    return (lat[0] + lat[1]).astype(dt), y_ctx


def neighbourhood_attend(q, k, v, k_ctx, v_ctx, rpb):
    bsz, seq, nh, hd = q.shape
    rows = seq // GRID_W
    wr = min(NA_ROWS, rows)
    q = q * (hd ** -0.5)
    qg = q.reshape(bsz, rows, GRID_W, nh, hd)
    kg = k.reshape(bsz, rows, GRID_W, nh, hd)
    vg = v.reshape(bsz, rows, GRID_W, nh, hd)
    row_start = jnp.clip(jnp.arange(rows) - wr // 2, 0, rows - wr)
    col_start = jnp.clip(jnp.arange(GRID_W) - NA_COLS // 2, 0, GRID_W - NA_COLS)
    col_idx = col_start[:, None] + jnp.arange(NA_COLS)[None, :]
    col_off = col_idx - jnp.arange(GRID_W)[:, None] + (NA_COLS - 1)
    rpb_col = rpb[:, :, col_off]
    n_loc = wr * NA_COLS

    def row_block(args):
        r, q_row = args
        r0 = row_start[r]
        k_rows = lax.dynamic_slice_in_dim(kg, r0, wr, axis=1)
        v_rows = lax.dynamic_slice_in_dim(vg, r0, wr, axis=1)
        k_win = k_rows[:, :, col_idx]
        v_win = v_rows[:, :, col_idx]
        row_off = r0 + jnp.arange(wr) - r + (NA_ROWS - 1)
        bias = jnp.transpose(rpb_col[:, row_off], (2, 0, 1, 3))[None]
        s_loc = jnp.einsum('bjhd,bwjmhd->bjhwm', q_row, k_win) + bias
        s_ctx = jnp.einsum('bjhd,bchd->bjhc', q_row, k_ctx)
        s = jnp.concatenate([s_loc.reshape(bsz, GRID_W, nh, n_loc), s_ctx], -1).astype(F32)
        p = jax.nn.softmax(s, axis=-1).astype(v.dtype)
        p_loc = p[..., :n_loc].reshape(bsz, GRID_W, nh, wr, NA_COLS)
        return (jnp.einsum('bjhwm,bwjmhd->bjhd', p_loc, v_win)
                + jnp.einsum('bjhc,bchd->bjhd', p[..., n_loc:], v_ctx))

    o = lax.map(row_block, (jnp.arange(rows), jnp.moveaxis(qg, 1, 0)))
    return jnp.moveaxis(o, 0, 1).reshape(bsz, seq, nh, hd)


def context_attend(q, k, v):
    s = jnp.einsum('bqhd,bkhd->bhqk', q * (q.shape[-1] ** -0.5), k).astype(F32)
    p = jax.nn.softmax(s, axis=-1).astype(v.dtype)
    return jnp.einsum('bhqk,bkhd->bqhd', p, v)


def axial_rope(t, rows_pos, cols_pos):
    half = t.shape[-1] // 2
    nf = half // 2
    inv_freq = ROPE_BASE ** (-jnp.arange(nf, dtype=F32) / nf)

    def rot(tp, pos):
        ang = pos.astype(F32)[:, None] * inv_freq[None, :]
        cos = jnp.cos(ang)[None, :, None, :]
        sin = jnp.sin(ang)[None, :, None, :]
        t1, t2 = tp[..., :nf], tp[..., nf:]
        return jnp.concatenate([t1 * cos - t2 * sin, t2 * cos + t1 * sin], -1)

    return jnp.concatenate([rot(t[..., :half], rows_pos), rot(t[..., half:], cols_pos)], -1)


def l2_normalise(t):
    return t * lax.rsqrt(jnp.sum(jnp.square(t), -1, keepdims=True) + NORM_EPS)


def gdn_qkv(qg, kg, vg, rope):
    q = l2_normalise(split_heads(jax.nn.silu(qg), GDN_HEADS).astype(F32))
    k = l2_normalise(split_heads(jax.nn.silu(kg), GDN_HEADS).astype(F32))
    v = split_heads(jax.nn.silu(vg), GDN_HEADS).astype(F32)
    if rope is not None:
        q = axial_rope(q, *rope)
        k = axial_rope(k, *rope)
    return q * (GDN_DK ** -0.5), k, v


def gdn_gates(b_raw, a_raw, a_log, dt_bias, d):
    sl = slice(d * GDN_HEADS, (d + 1) * GDN_HEADS)
    beta = jax.nn.sigmoid(b_raw[..., sl].astype(F32))
    g = -jnp.exp(a_log[d].astype(F32)) * jax.nn.softplus(a_raw[..., sl].astype(F32) + dt_bias[d].astype(F32))
    return beta, g


def gated_delta_chunked(q, k, v, beta, g, s0):
    bsz, seq, nh, _ = q.shape
    dv = v.shape[-1]
    n = seq // GDN_CHUNK

    def chunks(t):
        t = t.reshape((bsz, n, GDN_CHUNK) + t.shape[2:])
        return jnp.moveaxis(t, (1, 2), (0, 3))

    qc, kc, vc, bc, gc = (chunks(t) for t in (q, k, v, beta, g))
    gcum = jnp.cumsum(gc, axis=-1)
    lower = jnp.tril(jnp.ones((GDN_CHUNK, GDN_CHUNK), bool))
    strict = jnp.tril(jnp.ones((GDN_CHUNK, GDN_CHUNK), bool), -1)
    decay = jnp.exp(jnp.where(lower, gcum[..., :, None] - gcum[..., None, :], -jnp.inf))
    kb = kc * bc[..., None]
    a_mat = (jnp.where(strict, jnp.einsum('nbhid,nbhjd->nbhij', kb, kc) * decay, 0.0)
             + jnp.eye(GDN_CHUNK, dtype=F32))
    u = lax.linalg.triangular_solve(a_mat, vc * bc[..., None], left_side=True, lower=True, unit_diagonal=True)
    w = lax.linalg.triangular_solve(a_mat, kb * jnp.exp(gcum)[..., None], left_side=True, lower=True,
                                    unit_diagonal=True)
    qk = jnp.einsum('nbhid,nbhjd->nbhij', qc, kc) * decay

    def step(s, inp):
        q_i, k_i, u_i, w_i, qk_i, g_i = inp
        v_new = u_i - jnp.einsum('bhck,bhkv->bhcv', w_i, s)
        o = (jnp.einsum('bhck,bhkv->bhcv', q_i * jnp.exp(g_i)[..., None], s)
             + jnp.einsum('bhij,bhjv->bhiv', qk_i, v_new))
        g_last = g_i[..., -1:]
        s = (s * jnp.exp(g_last)[..., None]
             + jnp.einsum('bhck,bhcv->bhkv', k_i * jnp.exp(g_last - g_i)[..., None], v_new))
        return s, o

    s_fin, o = lax.scan(step, s0, (qc, kc, u, w, qk, gcum))
    o = jnp.moveaxis(o, (0, 3), (1, 2)).reshape(bsz, seq, nh, dv)
    return s_fin, o


def gdn_bidir(lat, ctx, a_log, dt_bias, ctx_out):
    q_l, k_l, v_l, b_l, a_l = lat
    q_c, k_c, v_c, b_c, a_c = ctx
    s0 = jnp.zeros((q_c.shape[0], GDN_HEADS, GDN_DK, GDN_DV), F32)
    outs_l, outs_c = [], []
    for d in range(2):
        beta_c, g_c = gdn_gates(b_c, a_c, a_log, dt_bias, d)
        beta_l, g_l = gdn_gates(b_l, a_l, a_log, dt_bias, d)
        s_c, o_c = gated_delta_chunked(*[flip_seq(t, d) for t in (q_c, k_c, v_c, beta_c, g_c)], s0)
        _, o_l = gated_delta_chunked(*[flip_seq(t, d) for t in (q_l, k_l, v_l, beta_l, g_l)], s_c)
        outs_l.append(flip_seq(o_l, d))
        outs_c.append(flip_seq(o_c, d))
    o_ctx = outs_c[0] + outs_c[1] if ctx_out else None
    return outs_l[0] + outs_l[1], o_ctx


def gdn_output(o, norm_w, z):
    o = o * lax.rsqrt(jnp.mean(jnp.square(o), -1, keepdims=True) + NORM_EPS) * norm_w.astype(F32)
    o = o * jax.nn.silu(split_heads(z, GDN_HEADS).astype(F32))
    return o.reshape(o.shape[:2] + (GDN_WIDTH,)).astype(z.dtype)


def merge_heads(t):
    return t.reshape(t.shape[:2] + (-1,))


def trunk_layer(x, xc, mod, mod_c, rope, w_in, conv_w, rg_wa, rg_ba, rg_wx, rg_bx, rg_lam, na_rpb,
                gdn_alog, gdn_dtb, gdn_nw, w_out, ln_g, ln_b, ctx_out):
    shift, scale, gate = jnp.split(mod, 3, axis=-1)
    shift_c, scale_c, gate_c = jnp.split(mod_c, 3, axis=-1)
    (a_l, qg_l, kg_l, vg_l, za_l, qn_l, kn_l, vn_l, zn_l, zg_l, br_l, ar_l) = combined_projection(
        x * (1 + scale[:, None]) + shift[:, None], w_in, conv_w)
    (a_c, qg_c, kg_c, vg_c, za_c, qn_c, kn_c, vn_c, zn_c, zg_c, br_c, ar_c) = combined_projection(
        xc * (1 + scale_c) + shift_c, w_in, conv_w)

    h_l, h_c = rglru_bidir(a_l, a_c, rg_wa, rg_ba, rg_wx, rg_bx, rg_lam, ctx_out)
    kb_c, vb_c = split_heads(kn_c, NA_HEADS), split_heads(vn_c, NA_HEADS)
    nb_l = neighbourhood_attend(split_heads(qn_l, NA_HEADS), split_heads(kn_l, NA_HEADS),
                                split_heads(vn_l, NA_HEADS), kb_c, vb_c, na_rpb)
    q_l, k_l, v_l = gdn_qkv(qg_l, kg_l, vg_l, rope)
    q_c, k_c, v_c = gdn_qkv(qg_c, kg_c, vg_c, None)
    o_l, o_c = gdn_bidir((q_l, k_l, v_l, br_l, ar_l), (q_c, k_c, v_c, br_c, ar_c), gdn_alog, gdn_dtb, ctx_out)

    y_l = jnp.concatenate([h_l * jax.nn.silu(za_l),
                           merge_heads(nb_l) * jax.nn.silu(zn_l),
                           gdn_output(o_l, gdn_nw, zg_l)], -1) @ w_out
    x_new = layer_norm(DEEPNORM_ALPHA * x + gate[:, None] * y_l, ln_g, ln_b)
    if not ctx_out:
        return x_new, None
    nb_c = context_attend(split_heads(qn_c, NA_HEADS), kb_c, vb_c)
    y_c = jnp.concatenate([h_c * jax.nn.silu(za_c),
                           merge_heads(nb_c) * jax.nn.silu(zn_c),
                           gdn_output(o_c, gdn_nw, zg_c)], -1) @ w_out
    xc_new = layer_norm(DEEPNORM_ALPHA * xc + gate_c * y_c, ln_g, ln_b)
    return x_new, xc_new


def setup_inputs(seed: int = 0) -> dict:
    key = jax.random.key(seed)
    ks = jax.random.split(key, 20)

    def nrm(k, shape, s):
        return jax.random.normal(k, shape, F32) * s

    x = nrm(ks[0], (BATCH, SEQ, D_MODEL), 1.0)
    c = nrm(ks[1], (BATCH, D_MODEL), 1.0)
    ctx = nrm(ks[2], (BATCH, CTX_LEN, D_MODEL), 1.0)
    c_ctx = nrm(ks[3], (D_MODEL,), 1.0)
    w_mod = nrm(ks[4], (DEPTH, D_MODEL, 3 * D_MODEL), 0.5 * D_MODEL ** -0.5)
    b_mod = nrm(ks[5], (DEPTH, 3 * D_MODEL), 0.02)
    w_in = nrm(ks[6], (DEPTH, D_MODEL, D_IN), D_MODEL ** -0.5)
    conv_w = nrm(ks[7], (DEPTH, CONV_K, CONV_CH), CONV_K ** -0.5)
    rg_wa = nrm(ks[8], (DEPTH, 2, RG_BLOCKS, RG_BLOCK, RG_BLOCK), RG_BLOCK ** -0.5)
    rg_ba = nrm(ks[9], (DEPTH, 2, RG_WIDTH), 0.02)
    rg_wx = nrm(ks[10], (DEPTH, 2, RG_BLOCKS, RG_BLOCK, RG_BLOCK), RG_BLOCK ** -0.5)
    rg_bx = nrm(ks[11], (DEPTH, 2, RG_WIDTH), 0.02)
    a_pow = jax.random.uniform(ks[12], (DEPTH, 2, RG_WIDTH), F32, 0.9, 0.999)
    s_lam = a_pow ** (1.0 / RG_C)
    rg_lam = jnp.log(s_lam) - jnp.log1p(-s_lam)
    na_rpb = nrm(ks[13], (DEPTH, NA_HEADS, 2 * NA_ROWS - 1, 2 * NA_COLS - 1), 0.1)
    gdn_alog = jnp.log(jax.random.uniform(ks[14], (DEPTH, 2, GDN_HEADS), F32, 1.0, 16.0))
    dt0 = jnp.exp(jax.random.uniform(ks[15], (DEPTH, 2, GDN_HEADS), F32, np.log(1e-3), np.log(1e-1)))
    gdn_dtb = dt0 + jnp.log(-jnp.expm1(-dt0))
    gdn_nw = 1.0 + nrm(ks[16], (DEPTH, GDN_DV), 0.02)
    w_out = nrm(ks[17], (DEPTH, D_MIX, D_MODEL), D_MIX ** -0.5 * DEEPNORM_BETA)
    ln_g = 1.0 + nrm(ks[18], (DEPTH, D_MODEL), 0.02)
    ln_b = nrm(ks[19], (DEPTH, D_MODEL), 0.02)
    return {'x': x, 'c': c, 'ctx': ctx, 'c_ctx': c_ctx, 'w_mod': w_mod, 'b_mod': b_mod, 'w_in': w_in,
            'conv_w': conv_w, 'rg_wa': rg_wa, 'rg_ba': rg_ba, 'rg_wx': rg_wx, 'rg_bx': rg_bx,
            'rg_lam': rg_lam, 'na_rpb': na_rpb, 'gdn_alog': gdn_alog, 'gdn_dtb': gdn_dtb,
            'gdn_nw': gdn_nw, 'w_out': w_out, 'ln_g': ln_g, 'ln_b': ln_b}


def reference(x, c, ctx, c_ctx, w_mod, b_mod, w_in, conv_w, rg_wa, rg_ba, rg_wx, rg_bx, rg_lam, na_rpb,
              gdn_alog, gdn_dtb, gdn_nw, w_out, ln_g, ln_b):
    pos = jnp.arange(x.shape[1])
    rope = (pos // GRID_W, pos % GRID_W)
    sc = jax.nn.silu(c)
    scc = jax.nn.silu(c_ctx)
    xc = ctx
    for l in range(DEPTH):
        mod = sc @ w_mod[l] + b_mod[l]
        mod_c = scc @ w_mod[l] + b_mod[l]
        x, xc = trunk_layer(x, xc, mod, mod_c, rope, w_in[l], conv_w[l], rg_wa[l], rg_ba[l], rg_wx[l],
                            rg_bx[l], rg_lam[l], na_rpb[l], gdn_alog[l], gdn_dtb[l], gdn_nw[l], w_out[l],
                            ln_g[l], ln_b[l], l < DEPTH - 1)
    return x
```

```python
import functools

import numpy as np
import jax
import jax.numpy as jnp
from jax import lax
from jax.experimental import pallas as pl
from jax.experimental.pallas import tpu as pltpu

F32 = jnp.float32
BF16 = jnp.bfloat16
HIGHEST = lax.Precision.HIGHEST

D = 1024
DEPTH = 4
LC = 256
GRID_W = 64
CONV_K = 4
RG_W = 384
RG_BLOCK = 64
RG_C = 8.0
NA_HEADS = 6
NA_W = 384
NA_ROWS = 8
NA_COLS = 16
GDN_HEADS = 4
GDN_W = 256
HD = 64
CHUNK = 64
ROPE_BASE = 10000.0
DEEPNORM_ALPHA = (2.0 * DEPTH) ** 0.25
LN_EPS = 1e-5
NORM_EPS = 1e-6
NEG = -1e30

C_RGX = 0
C_GQ = 384
C_GK = 640
C_GV = 896
C_ZA = 1152
C_NQ = 1536
C_NK = 1920
C_NV = 2304
C_ZN = 2688
C_ZG = 3072
C_GATE = 3328
D_IN = 3344
D_INP = 3456

LANE = 128
SUB = 8
TM = 256
RC = 128
VMEM_BIG = 56 * 1024 * 1024


def _silu(x):
    return x * jax.nn.sigmoid(x)


def _softplus(x):
    return jnp.maximum(x, 0.0) + jnp.log1p(jnp.exp(-jnp.abs(x)))


def _dot(a, b, precision=None):
    return jnp.dot(a, b, preferred_element_type=F32, precision=precision)


def _dot_nt(a, b, precision=None):
    return lax.dot_general(a, b, (((1,), (1,)), ((), ())), preferred_element_type=F32, precision=precision)


def _dot_tn(a, b, precision=None):
    return lax.dot_general(a, b, (((0,), (0,)), ((), ())), preferred_element_type=F32, precision=precision)


def _mod_kernel(c_ref, w_ref, b_ref, o_ref):
    s = _silu(c_ref[...])
    o_ref[...] = _dot(s, w_ref[...], HIGHEST) + b_ref[...]


def _modulation(c8, w_mod, b_mod):
    nb = 3 * D // D
    return pl.pallas_call(
        _mod_kernel,
        grid=(DEPTH, nb),
        in_specs=[pl.BlockSpec((SUB, D), lambda l, j: (0, 0)),
                  pl.BlockSpec((None, D, D), lambda l, j: (l, 0, j)),
                  pl.BlockSpec((None, 1, D), lambda l, j: (l, 0, j))],
        out_specs=pl.BlockSpec((None, SUB, D), lambda l, j: (l, 0, j)),
        out_shape=jax.ShapeDtypeStruct((DEPTH, SUB, 3 * D), F32),
        name="modulation",
    )(c8, w_mod, b_mod.reshape(DEPTH, 1, 3 * D))


def _inproj_kernel(x_ref, m_ref, w_ref, o_ref):
    m = m_ref[...]
    shift = m[:, :D]
    scale = m[:, D:2 * D]
    u = x_ref[...] * (1.0 + scale) + shift
    o_ref[...] = _dot(u.astype(BF16), w_ref[...])


def _in_proj(xa, modsel, w_in_p, layer):
    bsz, t, _ = xa.shape
    nt = t // TM
    return pl.pallas_call(
        _inproj_kernel,
        grid=(bsz, nt),
        in_specs=[pl.BlockSpec((None, TM, D), lambda b, i: (b, i, 0)),
                  pl.BlockSpec((None, None, 1, 3 * D), lambda b, i: (layer, 2 * b + jnp.minimum(i, 1), 0, 0)),
                  pl.BlockSpec((None, D, D_INP), lambda b, i: (layer, 0, 0))],
        out_specs=pl.BlockSpec((None, TM, D_INP), lambda b, i: (b, i, 0)),
        out_shape=jax.ShapeDtypeStruct((bsz, t, D_INP), F32),
        compiler_params=pltpu.CompilerParams(vmem_limit_bytes=VMEM_BIG),
        name="in_proj",
    )(xa, modsel, w_in_p)


PAD_ROWS = 3 * SUB


def _fill_padded(src_ref, pad_ref, t):
    zeros = jnp.zeros((SUB, LANE), F32)
    pad_ref[0:SUB, :] = zeros
    pad_ref[SUB + LC:2 * SUB + LC, :] = zeros
    pad_ref[t + 2 * SUB:t + 3 * SUB, :] = zeros

    def body(c, carry):
        src = pl.multiple_of(c * LC, LC)
        dst = pl.multiple_of(src + SUB + jnp.where(c >= 1, SUB, 0), SUB)
        pad_ref[pl.ds(dst, LC), :] = src_ref[pl.ds(src, LC), :]
        return carry

    lax.fori_loop(0, t // LC, body, 0)


def _conv_chunk(pad_ref, cw, t0):
    start = pl.multiple_of(t0 + jnp.where(t0 >= LC, SUB, 0), SUB)
    win = pad_ref[pl.ds(start, RC + 2 * SUB), :]
    n = RC + 2 * SUB
    acc = win[SUB:SUB + RC] * cw[2:3, :]
    for k, shift in ((0, 2), (1, 1), (3, n - 1)):
        acc = acc + pltpu.roll(win, shift, 0)[SUB:SUB + RC] * cw[k:k + 1, :]
    return acc


def _scan8(a, b, row, reverse):
    for s in (1, 2, 4):
        if reverse:
            sh, ok = SUB - s, row < SUB - s
        else:
            sh, ok = s, row >= s
        a_sh = jnp.where(ok, pltpu.roll(a, sh, 0), 1.0)
        b_sh = jnp.where(ok, pltpu.roll(b, sh, 0), 0.0)
        b = a * b_sh + b
        a = a * a_sh
    return a, b


def _rglru_kernel(x_ref, z_ref, cw_ref, w_ref, bias_ref, lam_ref, o_ref,
                  pad_ref, a0_ref, b0_ref, a1_ref, b1_ref, hf_ref, hb_ref):
    t = x_ref.shape[0]
    _fill_padded(x_ref, pad_ref, t)
    cw = cw_ref[...]
    w = w_ref[...]
    bias = bias_ref[...]
    ls = -_softplus(-lam_ref[...])

    def gates(c, carry):
        t0 = pl.multiple_of(c * RC, RC)
        u = _conv_chunk(pad_ref, cw, t0)
        g = _dot(u, w, HIGHEST) + bias
        for d, (a_ref, b_ref) in enumerate(((a0_ref, b0_ref), (a1_ref, b1_ref))):
            r = jax.nn.sigmoid(g[:, (2 * d) * LANE:(2 * d + 1) * LANE])
            i = jax.nn.sigmoid(g[:, (2 * d + 1) * LANE:(2 * d + 2) * LANE])
            log_a = RG_C * r * ls[:, d * LANE:(d + 1) * LANE]
            a = jnp.exp(log_a)
            mult = jnp.sqrt(-jnp.tanh(log_a) * (a * a + 1.0))
            a_ref[pl.ds(t0, RC), :] = a
            b_ref[pl.ds(t0, RC), :] = mult * (i * u)
        return carry

    lax.fori_loop(0, t // RC, gates, 0)

    row = lax.broadcasted_iota(jnp.int32, (SUB, LANE), 0)
    n_tiles = t // SUB
    n_ctx = LC // SUB

    def scan(i, carry):
        cf, cb = carry
        rf = pl.multiple_of(i * SUB, SUB)
        af, bf = _scan8(a0_ref[pl.ds(rf, SUB), :], b0_ref[pl.ds(rf, SUB), :], row, False)
        hf = bf + af * cf
        hf_ref[pl.ds(rf, SUB), :] = hf
        cf = jnp.broadcast_to(hf[SUB - 1:SUB, :], (SUB, LANE))
        j = jnp.where(i < n_ctx, n_ctx - 1 - i, n_tiles + n_ctx - 1 - i)
        rb = pl.multiple_of(j * SUB, SUB)
        ab, bb = _scan8(a1_ref[pl.ds(rb, SUB), :], b1_ref[pl.ds(rb, SUB), :], row, True)
        hb = bb + ab * cb
        hb_ref[pl.ds(rb, SUB), :] = hb
        cb = jnp.broadcast_to(hb[0:1, :], (SUB, LANE))
        return cf, cb

    zero = jnp.zeros((SUB, LANE), F32)
    lax.fori_loop(0, n_tiles, scan, (zero, zero), unroll=2)

    def finish(c, carry):
        t0 = pl.multiple_of(c * RC, RC)
        h = hf_ref[pl.ds(t0, RC), :] + hb_ref[pl.ds(t0, RC), :]
        o_ref[pl.ds(t0, RC), :] = h * _silu(z_ref[pl.ds(t0, RC), :])
        return carry

    lax.fori_loop(0, t // RC, finish, 0)


def _rglru(p, conv_w, rg_w, rg_b, rg_lam2, layer):
    bsz, t, _ = p.shape
    ns = RG_W // LANE
    seq = lambda: pltpu.VMEM((t, LANE), F32)
    return pl.pallas_call(
        _rglru_kernel,
        grid=(bsz, ns),
        in_specs=[pl.BlockSpec((None, t, LANE), lambda b, s: (b, 0, C_RGX // LANE + s)),
                  pl.BlockSpec((None, t, LANE), lambda b, s: (b, 0, C_ZA // LANE + s)),
                  pl.BlockSpec((None, CONV_K, LANE), lambda b, s: (layer, 0, C_RGX // LANE + s)),
                  pl.BlockSpec((None, None, LANE, 4 * LANE), lambda b, s: (layer, s, 0, 0)),
                  pl.BlockSpec((None, None, 1, 4 * LANE), lambda b, s: (layer, s, 0, 0)),
                  pl.BlockSpec((None, None, 1, 2 * LANE), lambda b, s: (layer, s, 0, 0))],
        out_specs=pl.BlockSpec((None, t, LANE), lambda b, s: (b, 0, s)),
        out_shape=jax.ShapeDtypeStruct((bsz, t, RG_W), F32),
        scratch_shapes=[pltpu.VMEM((t + PAD_ROWS, LANE), F32), seq(), seq(), seq(), seq(), seq(), seq()],
        compiler_params=pltpu.CompilerParams(vmem_limit_bytes=VMEM_BIG),
        name="rglru",
    )(p, p, conv_w, rg_w, rg_b, rg_lam2)


def _na_kernel(q_ref, k_ref, v_ref, z_ref, bias_ref, o_ref):
    j = pl.program_id(1)
    q = q_ref[...] * (HD ** -0.5)
    lane = lax.broadcasted_iota(jnp.int32, (GRID_W, LANE), 1)
    first = lane < HD
    n_loc = NA_ROWS * GRID_W

    def attend(local):
        if local:
            r = j - LC // GRID_W
            r0 = jnp.clip(r - NA_ROWS // 2, 0, GRID_W - NA_ROWS)
            start = pl.multiple_of(LC + r0 * GRID_W, GRID_W)
        for s in range(NA_W // LANE):
            cols = slice(s * LANE, (s + 1) * LANE)
            kc = k_ref[0:LC, cols].astype(BF16)
            vc = v_ref[0:LC, cols].astype(BF16)
            if local:
                kl = k_ref[pl.ds(start, n_loc), cols].astype(BF16)
                vl = v_ref[pl.ds(start, n_loc), cols].astype(BF16)
            qs = q[:, cols]
            outs = []
            for hh in range(2):
                qm = jnp.where(first if hh == 0 else ~first, qs, 0.0).astype(BF16)
                s_ctx = _dot_nt(qm, kc)
                m = jnp.max(s_ctx, axis=-1, keepdims=True)
                if local:
                    s_loc = _dot_nt(qm, kl) + bias_ref[2 * s + hh]
                    m = jnp.maximum(m, jnp.max(s_loc, axis=-1, keepdims=True))
                    p_loc = jnp.exp(s_loc - m)
                p_ctx = jnp.exp(s_ctx - m)
                den = jnp.sum(p_ctx, axis=-1, keepdims=True)
                pv = _dot(p_ctx.astype(BF16), vc)
                if local:
                    den = den + jnp.sum(p_loc, axis=-1, keepdims=True)
                    pv = pv + _dot(p_loc.astype(BF16), vl)
                outs.append(pv / den)
            o = jnp.where(first, outs[0], outs[1])
            o_ref[:, cols] = o * _silu(z_ref[:, cols])

    @pl.when(j >= LC // GRID_W)
    def _():
        attend(True)

    @pl.when(j < LC // GRID_W)
    def _():
        attend(False)


def _na_dr(j):
    r = jnp.maximum(j - LC // GRID_W, 0)
    return r - jnp.clip(r - NA_ROWS // 2, 0, GRID_W - NA_ROWS)


def _natten(p, bias_tab, layer):
    bsz, t, _ = p.shape
    nq = t // GRID_W
    return pl.pallas_call(
        _na_kernel,
        grid=(bsz, nq),
        in_specs=[pl.BlockSpec((None, GRID_W, NA_W), lambda b, j: (b, j, C_NQ // NA_W)),
                  pl.BlockSpec((None, t, NA_W), lambda b, j: (b, 0, C_NK // NA_W)),
                  pl.BlockSpec((None, t, NA_W), lambda b, j: (b, 0, C_NV // NA_W)),
                  pl.BlockSpec((None, GRID_W, NA_W), lambda b, j: (b, j, C_ZN // NA_W)),
                  pl.BlockSpec((None, None, NA_HEADS, GRID_W, NA_ROWS * GRID_W),
                               lambda b, j: (layer, _na_dr(j), 0, 0, 0))],
        out_specs=pl.BlockSpec((None, GRID_W, NA_W), lambda b, j: (b, j, 0)),
        out_shape=jax.ShapeDtypeStruct((bsz, t, NA_W), F32),
        compiler_params=pltpu.CompilerParams(vmem_limit_bytes=VMEM_BIG),
        name="natten",
    )(p, p, p, p, bias_tab)


def _swap16(x):
    lane = lax.broadcasted_iota(jnp.int32, x.shape, 1)
    return jnp.where(lane % 32 < 16, pltpu.roll(x, LANE - 16, 1), pltpu.roll(x, 16, 1))


def _gdn_prep_kernel(q_ref, k_ref, v_ref, cw_ref, cos_ref, sin_ref, hsum_ref, qo_ref, ko_ref, vo_ref, pad_ref):
    t = q_ref.shape[0]
    hsum = hsum_ref[...]

    def one(src_ref, dst_ref, cw, mode):
        _fill_padded(src_ref, pad_ref, t)

        def body(c, carry):
            t0 = pl.multiple_of(c * RC, RC)
            x = _silu(_conv_chunk(pad_ref, cw, t0))
            if mode != "v":
                ssq = _dot(x * x, hsum, HIGHEST)
                x = x * lax.rsqrt(ssq + NORM_EPS)
            dst_ref[pl.ds(t0, RC), :] = x
            return carry

        lax.fori_loop(0, t // RC, body, 0)
        if mode == "v":
            return

        def rope(c, carry):
            t0 = pl.multiple_of(LC + c * RC, RC)
            r0 = pl.multiple_of(c * RC, RC)
            x = dst_ref[pl.ds(t0, RC), :]
            x = x * cos_ref[pl.ds(r0, RC), :] + _swap16(x) * sin_ref[pl.ds(r0, RC), :]
            dst_ref[pl.ds(t0, RC), :] = x
            return carry

        lax.fori_loop(0, (t - LC) // RC, rope, 0)
        if mode == "q":
            def scale(c, carry):
                t0 = pl.multiple_of(c * RC, RC)
                dst_ref[pl.ds(t0, RC), :] = dst_ref[pl.ds(t0, RC), :] * (HD ** -0.5)
                return carry

            lax.fori_loop(0, t // RC, scale, 0)

    one(q_ref, qo_ref, cw_ref[0], "q")
    one(k_ref, ko_ref, cw_ref[1], "k")
    one(v_ref, vo_ref, cw_ref[2], "v")


def _gdn_prep(p, conv_w3, cos_t, sin_t, hsum, layer):
    bsz, t, _ = p.shape
    ns = GDN_W // LANE
    col = lambda c0: (lambda b, s: (b, 0, c0 // LANE + s))
    out = jax.ShapeDtypeStruct((bsz, t, GDN_W), F32)
    return pl.pallas_call(
        _gdn_prep_kernel,
        grid=(bsz, ns),
        in_specs=[pl.BlockSpec((None, t, LANE), col(C_GQ)),
                  pl.BlockSpec((None, t, LANE), col(C_GK)),
                  pl.BlockSpec((None, t, LANE), col(C_GV)),
                  pl.BlockSpec((None, 3, CONV_K, LANE), lambda b, s: (layer, 0, 0, s)),
                  pl.BlockSpec((t - LC, LANE), lambda b, s: (0, 0)),
                  pl.BlockSpec((t - LC, LANE), lambda b, s: (0, 0)),
                  pl.BlockSpec((LANE, LANE), lambda b, s: (0, 0))],
        out_specs=[pl.BlockSpec((None, t, LANE), lambda b, s: (b, 0, s))] * 3,
        out_shape=[out, out, out],
        scratch_shapes=[pltpu.VMEM((t + PAD_ROWS, LANE), F32)],
        compiler_params=pltpu.CompilerParams(vmem_limit_bytes=VMEM_BIG),
        name="gdn_prep",
    )(p, p, p, conv_w3, cos_t, sin_t, hsum)


GB = 256
NCH = GB // CHUNK


def _gdn_scan_kernel(qf_ref, kf_ref, vf_ref, gf_ref, qb_ref, kb_ref, vb_ref, gb_ref,
                     esel_ref, gvec_ref, of_ref, ob_ref,
                     s_ref, u_ref, w_ref, qk_ref, qg_ref, kg_ref, eg_ref):
    step = pl.program_id(1)
    w4 = GDN_W

    @pl.when(step == 0)
    def _():
        s_ref[...] = jnp.zeros_like(s_ref)

    row = lax.broadcasted_iota(jnp.int32, (CHUNK, w4), 0)
    jl = lax.broadcasted_iota(jnp.int32, (CHUNK, w4), 1) % HD
    r2 = lax.broadcasted_iota(jnp.int32, (w4, w4), 0) // HD
    c2 = lax.broadcasted_iota(jnp.int32, (w4, w4), 1) // HD
    bd = r2 == c2
    ti = lax.broadcasted_iota(jnp.int32, (CHUNK, CHUNK), 0)
    tj = lax.broadcasted_iota(jnp.int32, (CHUNK, CHUNK), 1)
    eye = (jl == row).astype(F32)
    blk16 = (row // 16) == (jl // 16)
    blk32 = (row // 32) == (jl // 32)
    alog = gvec_ref[0:1, :]
    dtb = gvec_ref[1:2, :]
    glane = lax.broadcasted_iota(jnp.int32, (GB, LANE), 1)

    def expand(y):
        return jnp.where(bd, jnp.concatenate([y, y, y, y], axis=0), 0.0)

    def mm(x, y):
        return _dot(x, expand(y), HIGHEST)

    refs = ((qf_ref, kf_ref, vf_ref, gf_ref), (qb_ref, kb_ref, vb_ref, gb_ref))
    for d in range(2):
        q_ref, k_ref, v_ref, g_ref = refs[d]
        raw = g_ref[...]
        comp = jnp.where(glane < 2 * GDN_HEADS, jax.nn.sigmoid(raw), -jnp.exp(alog) * _softplus(raw + dtb))
        gexp = _dot(comp, esel_ref[d], HIGHEST)
        if d == 0:
            incl, strict, upto, tri = jl <= row, jl < row, row <= jl, (tj <= ti).astype(F32)
        else:
            incl, strict, upto, tri = jl >= row, jl > row, row >= jl, (tj >= ti).astype(F32)
        last = CHUNK - 1 if d == 0 else 0
        for c in range(NCH):
            rows = slice(c * CHUNK, (c + 1) * CHUNK)
            q = q_ref[rows, :]
            k = k_ref[rows, :]
            v = v_ref[rows, :]
            beta = gexp[rows, 0:w4]
            g = gexp[rows, w4:2 * w4]
            gcum = _dot(tri, g, HIGHEST)
            grow = jnp.sum(jnp.where(upto, g, 0.0), axis=0, keepdims=True)
            decay = jnp.where(incl, jnp.exp(jnp.minimum(gcum - grow, 0.0)), 0.0)
            kbeta = k * beta
            kbd = jnp.where(bd, jnp.concatenate([k, k, k, k], axis=0).T, 0.0)
            prod = _dot(jnp.concatenate([kbeta, q], axis=0), kbd, HIGHEST)
            n = jnp.where(strict, prod[0:CHUNK] * decay, 0.0)
            qk_ref[d, rows, :] = jnp.where(incl, prod[CHUNK:2 * CHUNK] * decay, 0.0)
            n16 = jnp.where(blk16, n, 0.0)
            tinv = eye - n16
            pw = mm(n16, n16)
            tinv = tinv + mm(tinv, pw)
            pw = mm(pw, pw)
            tinv = tinv + mm(tinv, pw)
            pw = mm(pw, pw)
            tinv = tinv + mm(tinv, pw)
            e32 = jnp.where(blk32 & ~blk16, n, 0.0)
            tinv = tinv - mm(mm(tinv, e32), tinv)
            e64 = jnp.where(blk32, 0.0, n)
            tinv = tinv - mm(mm(tinv, e64), tinv)
            eg = jnp.exp(gcum)
            u_ref[d, rows, :] = mm(tinv, v * beta)
            w_ref[d, rows, :] = mm(tinv, kbeta * eg)
            glast = gcum[last:last + 1, :]
            qg_ref[d, rows, :] = q * eg
            kg_ref[d, rows, :] = k * jnp.exp(glast - gcum)
            eg_ref[d, c * SUB:(c + 1) * SUB, :] = jnp.broadcast_to(jnp.exp(glast), (SUB, w4))

    outs = (of_ref, ob_ref)
    for d in range(2):
        s = s_ref[d]
        order = range(NCH) if d == 0 else range(NCH - 1, -1, -1)
        for c in order:
            rows = slice(c * CHUNK, (c + 1) * CHUNK)
            v_new = u_ref[d, rows, :] - _dot(w_ref[d, rows, :], s, HIGHEST)
            o = _dot(qg_ref[d, rows, :], s, HIGHEST) + mm(qk_ref[d, rows, :], v_new)
            outs[d][rows, :] = o
            upd = _dot_tn(kg_ref[d, rows, :], v_new, HIGHEST)
            s = s * eg_ref[d, c * SUB:c * SUB + 1, :] + jnp.where(bd, upd, 0.0)
        s_ref[d] = s


def _gdn_scan(qn, kn, vn, p, esel, gvec, layer):
    bsz, t, _ = p.shape
    nsteps = t // GB
    fwd = lambda b, i: (b, i, 0)
    bwd = lambda b, i: (b, jnp.where(i == 0, 0, nsteps - i), 0)
    gcol = C_GATE // LANE
    fwd_g = lambda b, i: (b, i, gcol)
    bwd_g = lambda b, i: (b, jnp.where(i == 0, 0, nsteps - i), gcol)
    blk = lambda im: pl.BlockSpec((None, GB, GDN_W), im)
    out = jax.ShapeDtypeStruct((bsz, t, GDN_W), F32)
    dseq = lambda: pltpu.VMEM((2, GB, GDN_W), F32)
    return pl.pallas_call(
        _gdn_scan_kernel,
        grid=(bsz, nsteps),
        in_specs=[blk(fwd), blk(fwd), blk(fwd), pl.BlockSpec((None, GB, LANE), fwd_g),
                  blk(bwd), blk(bwd), blk(bwd), pl.BlockSpec((None, GB, LANE), bwd_g),
                  pl.BlockSpec((2, LANE, 2 * GDN_W), lambda b, i: (0, 0, 0)),
                  pl.BlockSpec((None, 2, LANE), lambda b, i: (layer, 0, 0))],
        out_specs=[blk(fwd), blk(bwd)],
        out_shape=[out, out],
        scratch_shapes=[pltpu.VMEM((2, GDN_W, GDN_W), F32), dseq(), dseq(), dseq(), dseq(), dseq(),
                        pltpu.VMEM((2, NCH * SUB, GDN_W), F32)],
        compiler_params=pltpu.CompilerParams(vmem_limit_bytes=VMEM_BIG),
        name="gdn_scan",
    )(qn, kn, vn, p, qn, kn, vn, p, esel, gvec)


def _outproj_kernel(x_ref, h_ref, nb_ref, of_ref, ob_ref, zg_ref, m_ref, w_ref, nw_ref, hsum_ref,
                    lng_ref, lnb_ref, o_ref):
    o = of_ref[...] + ob_ref[...]
    ms = _dot(o * o, hsum_ref[...], HIGHEST) * (1.0 / HD)
    og = o * lax.rsqrt(ms + NORM_EPS) * nw_ref[...] * _silu(zg_ref[...])
    y = (_dot(h_ref[...].astype(BF16), w_ref[0:RG_W, :])
         + _dot(nb_ref[...].astype(BF16), w_ref[RG_W:RG_W + NA_W, :])
         + _dot(og.astype(BF16), w_ref[RG_W + NA_W:D, :]))
    gate = m_ref[...][:, 2 * D:3 * D]
    z = DEEPNORM_ALPHA * x_ref[...] + gate * y
    mu = jnp.mean(z, axis=-1, keepdims=True)
    zc = z - mu
    var = jnp.mean(zc * zc, axis=-1, keepdims=True)
    o_ref[...] = zc * lax.rsqrt(var + LN_EPS) * lng_ref[...] + lnb_ref[...]


def _out_proj(xa, hg, nb, o_f, o_b, p, modsel, w_out_b, nw4, hsum4, ln_g, ln_b, layer):
    bsz, t, _ = xa.shape
    nt = t // TM
    row = lambda width: pl.BlockSpec((None, TM, width), lambda b, i: (b, i, 0))
    vec = lambda: pl.BlockSpec((None, 1, D), lambda b, i: (layer, 0, 0))
    return pl.pallas_call(
        _outproj_kernel,
        grid=(bsz, nt),
        in_specs=[row(D), row(RG_W), row(NA_W), row(GDN_W), row(GDN_W),
                  pl.BlockSpec((None, TM, GDN_W), lambda b, i: (b, i, C_ZG // GDN_W)),
                  pl.BlockSpec((None, None, 1, 3 * D), lambda b, i: (layer, 2 * b + jnp.minimum(i, 1), 0, 0)),
                  pl.BlockSpec((None, D, D), lambda b, i: (layer, 0, 0)),
                  pl.BlockSpec((None, 1, GDN_W), lambda b, i: (layer, 0, 0)),
                  pl.BlockSpec((GDN_W, GDN_W), lambda b, i: (0, 0)),
                  vec(), vec()],
        out_specs=row(D),
        out_shape=jax.ShapeDtypeStruct((bsz, t, D), F32),
        compiler_params=pltpu.CompilerParams(vmem_limit_bytes=VMEM_BIG),
        name="out_proj",
    )(xa, hg, nb, o_f, o_b, p, modsel, w_out_b, nw4, hsum4, ln_g, ln_b)


def _na_bias_table(na_rpb):
    jq = np.arange(GRID_W)
    col_start = np.clip(jq - NA_COLS // 2, 0, GRID_W - NA_COLS)
    kc = np.arange(GRID_W)
    inside = (kc[None, :] >= col_start[:, None]) & (kc[None, :] < col_start[:, None] + NA_COLS)
    col_off = np.clip(kc[None, :] - jq[:, None] + NA_COLS - 1, 0, 2 * NA_COLS - 2)
    dr = np.arange(NA_ROWS)
    wr = np.arange(NA_ROWS)
    row_off = wr[None, :] - dr[:, None] + NA_ROWS - 1
    tab = na_rpb[:, :, row_off][:, :, :, :, col_off]
    tab = jnp.where(inside[None, None, None, None], tab, NEG)
    tab = jnp.transpose(tab, (0, 2, 1, 4, 3, 5))
    return tab.reshape(DEPTH, NA_ROWS, NA_HEADS, GRID_W, NA_ROWS * GRID_W)


def _rope_tables(seq):
    pos = jnp.arange(seq)
    rows_pos, cols_pos = pos // GRID_W, pos % GRID_W
    half = HD // 2
    nf = half // 2
    inv_freq = ROPE_BASE ** (-jnp.arange(nf, dtype=F32) / nf)
    lane = np.arange(LANE)
    jl = lane % HD
    use_row = jl < half
    f = (jl % half) % nf
    sign = np.where((jl % half) < nf, -1.0, 1.0).astype(np.float32)
    pos_l = jnp.where(use_row[None, :], rows_pos[:, None], cols_pos[:, None]).astype(F32)
    ang = pos_l * inv_freq[f][None, :]
    return jnp.cos(ang), jnp.sin(ang) * sign[None, :]


def _gate_select():
    e = np.zeros((2, LANE, 2 * GDN_W), np.float32)
    for d in range(2):
        for l in range(GDN_W):
            h = l // HD
            e[d, d * GDN_HEADS + h, l] = 1.0
            e[d, 2 * GDN_HEADS + d * GDN_HEADS + h, GDN_W + l] = 1.0
    return jnp.asarray(e)


def _head_sum(width):
    i = np.arange(width)
    return jnp.asarray((i[:, None] // HD == i[None, :] // HD).astype(np.float32))


def kernel(x, c, ctx, c_ctx, w_mod, b_mod, w_in, conv_w, rg_wa, rg_ba, rg_wx, rg_bx, rg_lam, na_rpb, gdn_alog,
           gdn_dtb, gdn_nw, w_out, ln_g, ln_b):
    bsz, seq, _ = x.shape
    assert ctx.shape[1] == LC and seq == GRID_W * GRID_W

    c8 = jnp.concatenate([c, c_ctx[None], jnp.zeros((SUB - bsz - 1, D), F32)], axis=0)
    w_in_p = jnp.pad(w_in, ((0, 0), (0, 0), (0, D_INP - D_IN))).astype(BF16)
    w_out_b = w_out.astype(BF16)
    ns = RG_W // LANE
    eye2 = jnp.eye(2, dtype=F32)

    def block_diag(wt):
        wt = wt.reshape(DEPTH, 2, ns, 2, RG_BLOCK, RG_BLOCK)
        return jnp.einsum('ldsjae,jk->ldsjake', wt, eye2).reshape(DEPTH, 2, ns, LANE, LANE)

    wa, wx = block_diag(rg_wa), block_diag(rg_wx)
    rg_w = jnp.concatenate([wa[:, 0], wx[:, 0], wa[:, 1], wx[:, 1]], axis=-1)
    slab = lambda v: v.reshape(DEPTH, ns, 1, LANE)
    rg_b = jnp.concatenate([slab(rg_ba[:, 0]), slab(rg_bx[:, 0]), slab(rg_ba[:, 1]), slab(rg_bx[:, 1])], axis=-1)
    rg_lam2 = jnp.concatenate([slab(rg_lam[:, 0]), slab(rg_lam[:, 1])], axis=-1)
    conv_w3 = jnp.stack([conv_w[:, :, C_GQ:C_GK], conv_w[:, :, C_GK:C_GV], conv_w[:, :, C_GV:C_ZA]], axis=1)
    bias_tab = _na_bias_table(na_rpb)
    cos_t, sin_t = _rope_tables(seq)
    esel = _gate_select()
    pad = jnp.zeros((DEPTH, LANE - 4 * GDN_HEADS), F32)
    zero8 = jnp.zeros((DEPTH, 2 * GDN_HEADS), F32)
    gvec = jnp.stack([jnp.concatenate([zero8, gdn_alog.reshape(DEPTH, -1), pad], axis=-1),
                      jnp.concatenate([zero8, gdn_dtb.reshape(DEPTH, -1), pad], axis=-1)], axis=1)
    nw4 = jnp.tile(gdn_nw, (1, GDN_HEADS)).reshape(DEPTH, 1, GDN_W)
    hsum2, hsum4 = _head_sum(LANE), _head_sum(GDN_W)
    ln_g3, ln_b3 = ln_g.reshape(DEPTH, 1, D), ln_b.reshape(DEPTH, 1, D)

    mods = _modulation(c8, w_mod, b_mod)
    ctx_rows = jnp.broadcast_to(mods[:, bsz:bsz + 1], (DEPTH, bsz, 3 * D))
    modsel = jnp.stack([ctx_rows, mods[:, :bsz]], axis=2).reshape(DEPTH, 2 * bsz, 1, 3 * D)

    xa = jnp.concatenate([ctx, x], axis=1)
    for layer in range(DEPTH):
        p = _in_proj(xa, modsel, w_in_p, layer)
        hg = _rglru(p, conv_w, rg_w, rg_b, rg_lam2, layer)
        nb = _natten(p, bias_tab, layer)
        qn, kn, vn = _gdn_prep(p, conv_w3, cos_t, sin_t, hsum2, layer)
        o_f, o_b = _gdn_scan(qn, kn, vn, p, esel, gvec, layer)
        xa = _out_proj(xa, hg, nb, o_f, o_b, p, modsel, w_out_b, nw4, hsum4, ln_g3, ln_b3, layer)
    return xa[:, LC:]
```

```python
import functools

import numpy as np
import jax
import jax.numpy as jnp
from jax import lax
from jax.experimental import pallas as pl
from jax.experimental.pallas import tpu as pltpu

F32 = jnp.float32
BF16 = jnp.bfloat16
HIGHEST = lax.Precision.HIGHEST

D = 1024
DEPTH = 4
LC = 256
GRID_W = 64
CONV_K = 4
RG_W = 384
RG_BLOCK = 64
RG_C = 8.0
NA_HEADS = 6
NA_W = 384
NA_ROWS = 8
NA_COLS = 16
GDN_HEADS = 4
GDN_W = 256
HD = 64
CHUNK = 64
ROPE_BASE = 10000.0
DEEPNORM_ALPHA = (2.0 * DEPTH) ** 0.25
LN_EPS = 1e-5
NORM_EPS = 1e-6
NEG = -1e30

C_RGX = 0
C_GQ = 384
C_GK = 640
C_GV = 896
C_ZA = 1152
C_NQ = 1536
C_NK = 1920
C_NV = 2304
C_ZN = 2688
C_ZG = 3072
C_GATE = 3328
D_IN = 3344
D_INP = 3456

LANE = 128
SUB = 8
TM = 256
RC = 128
VMEM_BIG = 56 * 1024 * 1024


def _silu(x):
    return x * jax.nn.sigmoid(x)


def _softplus(x):
    return jnp.maximum(x, 0.0) + jnp.log1p(jnp.exp(-jnp.abs(x)))


def _dot(a, b, precision=None):
    return jnp.dot(a, b, preferred_element_type=F32, precision=precision)


def _dot_nt(a, b, precision=None):
    return lax.dot_general(a, b, (((1,), (1,)), ((), ())), preferred_element_type=F32, precision=precision)


def _dot_tn(a, b, precision=None):
    return lax.dot_general(a, b, (((0,), (0,)), ((), ())), preferred_element_type=F32, precision=precision)


def _mod_kernel(c_ref, w_ref, b_ref, o_ref):
    s = _silu(c_ref[...])
    o_ref[...] = _dot(s, w_ref[...], HIGHEST) + b_ref[...]


def _modulation(c8, w_mod, b_mod):
    nb = 3 * D // D
    return pl.pallas_call(
        _mod_kernel,
        grid=(DEPTH, nb),
        in_specs=[pl.BlockSpec((SUB, D), lambda l, j: (0, 0)),
                  pl.BlockSpec((None, D, D), lambda l, j: (l, 0, j)),
                  pl.BlockSpec((None, 1, D), lambda l, j: (l, 0, j))],
        out_specs=pl.BlockSpec((None, SUB, D), lambda l, j: (l, 0, j)),
        out_shape=jax.ShapeDtypeStruct((DEPTH, SUB, 3 * D), F32),
        name="modulation",
    )(c8, w_mod, b_mod.reshape(DEPTH, 1, 3 * D))


def _inproj_kernel(x_ref, m_ref, w_ref, o_ref):
    m = m_ref[...]
    shift = m[:, :D]
    scale = m[:, D:2 * D]
    u = x_ref[...] * (1.0 + scale) + shift
    o_ref[...] = _dot(u.astype(BF16), w_ref[...])


def _in_proj(xa, modsel, w_in_p, layer):
    bsz, t, _ = xa.shape
    nt = t // TM
    return pl.pallas_call(
        _inproj_kernel,
        grid=(bsz, nt),
        in_specs=[pl.BlockSpec((None, TM, D), lambda b, i: (b, i, 0)),
                  pl.BlockSpec((None, None, 1, 3 * D), lambda b, i: (layer, 2 * b + jnp.minimum(i, 1), 0, 0)),
                  pl.BlockSpec((None, D, D_INP), lambda b, i: (layer, 0, 0))],
        out_specs=pl.BlockSpec((None, TM, D_INP), lambda b, i: (b, i, 0)),
        out_shape=jax.ShapeDtypeStruct((bsz, t, D_INP), F32),
        compiler_params=pltpu.CompilerParams(vmem_limit_bytes=VMEM_BIG),
        name="in_proj",
    )(xa, modsel, w_in_p)


PAD_ROWS = 3 * SUB


def _fill_padded(src_ref, pad_ref, t):
    zeros = jnp.zeros((SUB, LANE), F32)
    pad_ref[0:SUB, :] = zeros
    pad_ref[SUB + LC:2 * SUB + LC, :] = zeros
    pad_ref[t + 2 * SUB:t + 3 * SUB, :] = zeros

    def body(c, carry):
        src = pl.multiple_of(c * LC, LC)
        dst = pl.multiple_of(src + SUB + jnp.where(c >= 1, SUB, 0), SUB)
        pad_ref[pl.ds(dst, LC), :] = src_ref[pl.ds(src, LC), :]
        return carry

    lax.fori_loop(0, t // LC, body, 0)


def _conv_chunk(pad_ref, cw, t0):
    start = pl.multiple_of(t0 + jnp.where(t0 >= LC, SUB, 0), SUB)
    win = pad_ref[pl.ds(start, RC + 2 * SUB), :]
    n = RC + 2 * SUB
    acc = win[SUB:SUB + RC] * cw[2:3, :]
    for k, shift in ((0, 2), (1, 1), (3, n - 1)):
        acc = acc + pltpu.roll(win, shift, 0)[SUB:SUB + RC] * cw[k:k + 1, :]
    return acc


def _scan8(a, b, row, reverse):
    for s in (1, 2, 4):
        if reverse:
            sh, ok = SUB - s, row < SUB - s
        else:
            sh, ok = s, row >= s
        a_sh = jnp.where(ok, pltpu.roll(a, sh, 0), 1.0)
        b_sh = jnp.where(ok, pltpu.roll(b, sh, 0), 0.0)
        b = a * b_sh + b
        a = a * a_sh
    return a, b


def _rglru_kernel(x_ref, z_ref, cw_ref, w_ref, bias_ref, lam_ref, o_ref,
                  pad_ref, a0_ref, b0_ref, a1_ref, b1_ref, hf_ref, hb_ref):
    t = x_ref.shape[0]
    _fill_padded(x_ref, pad_ref, t)
    cw = cw_ref[...]
    w = w_ref[...]
    bias = bias_ref[...]
    ls = -_softplus(-lam_ref[...])

    def gates(c, carry):
        t0 = pl.multiple_of(c * RC, RC)
        u = _conv_chunk(pad_ref, cw, t0)
        g = _dot(u, w, HIGHEST) + bias
        for d, (a_ref, b_ref) in enumerate(((a0_ref, b0_ref), (a1_ref, b1_ref))):
            r = jax.nn.sigmoid(g[:, (2 * d) * LANE:(2 * d + 1) * LANE])
            i = jax.nn.sigmoid(g[:, (2 * d + 1) * LANE:(2 * d + 2) * LANE])
            log_a = RG_C * r * ls[:, d * LANE:(d + 1) * LANE]
            a = jnp.exp(log_a)
            mult = jnp.sqrt(-jnp.tanh(log_a) * (a * a + 1.0))
            a_ref[pl.ds(t0, RC), :] = a
            b_ref[pl.ds(t0, RC), :] = mult * (i * u)
        return carry

    lax.fori_loop(0, t // RC, gates, 0)

    row = lax.broadcasted_iota(jnp.int32, (SUB, LANE), 0)
    n_tiles = t // SUB
    n_ctx = LC // SUB

    def scan(i, carry):
        cf, cb = carry
        rf = pl.multiple_of(i * SUB, SUB)
        af, bf = _scan8(a0_ref[pl.ds(rf, SUB), :], b0_ref[pl.ds(rf, SUB), :], row, False)
        hf = bf + af * cf
        hf_ref[pl.ds(rf, SUB), :] = hf
        cf = jnp.broadcast_to(hf[SUB - 1:SUB, :], (SUB, LANE))
        j = jnp.where(i < n_ctx, n_ctx - 1 - i, n_tiles + n_ctx - 1 - i)
        rb = pl.multiple_of(j * SUB, SUB)
        ab, bb = _scan8(a1_ref[pl.ds(rb, SUB), :], b1_ref[pl.ds(rb, SUB), :], row, True)
        hb = bb + ab * cb
        hb_ref[pl.ds(rb, SUB), :] = hb
        cb = jnp.broadcast_to(hb[0:1, :], (SUB, LANE))
        return cf, cb

    zero = jnp.zeros((SUB, LANE), F32)
    lax.fori_loop(0, n_tiles, scan, (zero, zero), unroll=2)

    def finish(c, carry):
        t0 = pl.multiple_of(c * RC, RC)
        h = hf_ref[pl.ds(t0, RC), :] + hb_ref[pl.ds(t0, RC), :]
        o_ref[pl.ds(t0, RC), :] = h * _silu(z_ref[pl.ds(t0, RC), :])
        return carry

    lax.fori_loop(0, t // RC, finish, 0)


def _rglru(p, conv_w, rg_w, rg_b, rg_lam2, layer):
    bsz, t, _ = p.shape
    ns = RG_W // LANE
    seq = lambda: pltpu.VMEM((t, LANE), F32)
    return pl.pallas_call(
        _rglru_kernel,
        grid=(bsz, ns),
        in_specs=[pl.BlockSpec((None, t, LANE), lambda b, s: (b, 0, C_RGX // LANE + s)),
                  pl.BlockSpec((None, t, LANE), lambda b, s: (b, 0, C_ZA // LANE + s)),
                  pl.BlockSpec((None, CONV_K, LANE), lambda b, s: (layer, 0, C_RGX // LANE + s)),
                  pl.BlockSpec((None, None, LANE, 4 * LANE), lambda b, s: (layer, s, 0, 0)),
                  pl.BlockSpec((None, None, 1, 4 * LANE), lambda b, s: (layer, s, 0, 0)),
                  pl.BlockSpec((None, None, 1, 2 * LANE), lambda b, s: (layer, s, 0, 0))],
        out_specs=pl.BlockSpec((None, t, LANE), lambda b, s: (b, 0, s)),
        out_shape=jax.ShapeDtypeStruct((bsz, t, RG_W), F32),
        scratch_shapes=[pltpu.VMEM((t + PAD_ROWS, LANE), F32), seq(), seq(), seq(), seq(), seq(), seq()],
        compiler_params=pltpu.CompilerParams(vmem_limit_bytes=VMEM_BIG),
        name="rglru",
    )(p, p, conv_w, rg_w, rg_b, rg_lam2)


def _na_kernel(q_ref, k_ref, v_ref, z_ref, bias_ref, o_ref):
    j = pl.program_id(1)
    q = q_ref[...] * (HD ** -0.5)
    lane = lax.broadcasted_iota(jnp.int32, (GRID_W, LANE), 1)
    first = lane < HD
    n_loc = NA_ROWS * GRID_W

    def attend(local):
        if local:
            r = j - LC // GRID_W
            r0 = jnp.clip(r - NA_ROWS // 2, 0, GRID_W - NA_ROWS)
            start = pl.multiple_of(LC + r0 * GRID_W, GRID_W)
        for s in range(NA_W // LANE):
            cols = slice(s * LANE, (s + 1) * LANE)
            kc = k_ref[0:LC, cols].astype(BF16)
            vc = v_ref[0:LC, cols].astype(BF16)
            if local:
                kl = k_ref[pl.ds(start, n_loc), cols].astype(BF16)
                vl = v_ref[pl.ds(start, n_loc), cols].astype(BF16)
            qs = q[:, cols]
            outs = []
            for hh in range(2):
                qm = jnp.where(first if hh == 0 else ~first, qs, 0.0).astype(BF16)
                s_ctx = _dot_nt(qm, kc)
                m = jnp.max(s_ctx, axis=-1, keepdims=True)
                if local:
                    s_loc = _dot_nt(qm, kl) + bias_ref[2 * s + hh]
                    m = jnp.maximum(m, jnp.max(s_loc, axis=-1, keepdims=True))
                    p_loc = jnp.exp(s_loc - m)
                p_ctx = jnp.exp(s_ctx - m)
                den = jnp.sum(p_ctx, axis=-1, keepdims=True)
                pv = _dot(p_ctx.astype(BF16), vc)
                if local:
                    den = den + jnp.sum(p_loc, axis=-1, keepdims=True)
                    pv = pv + _dot(p_loc.astype(BF16), vl)
                outs.append(pv / den)
            o = jnp.where(first, outs[0], outs[1])
            o_ref[:, cols] = o * _silu(z_ref[:, cols])

    @pl.when(j >= LC // GRID_W)
    def _():
        attend(True)

    @pl.when(j < LC // GRID_W)
    def _():
        attend(False)


def _na_dr(j):
    r = jnp.maximum(j - LC // GRID_W, 0)
    return r - jnp.clip(r - NA_ROWS // 2, 0, GRID_W - NA_ROWS)


def _natten(p, bias_tab, layer):
    bsz, t, _ = p.shape
    nq = t // GRID_W
    return pl.pallas_call(
        _na_kernel,
        grid=(bsz, nq),
        in_specs=[pl.BlockSpec((None, GRID_W, NA_W), lambda b, j: (b, j, C_NQ // NA_W)),
                  pl.BlockSpec((None, t, NA_W), lambda b, j: (b, 0, C_NK // NA_W)),
                  pl.BlockSpec((None, t, NA_W), lambda b, j: (b, 0, C_NV // NA_W)),
                  pl.BlockSpec((None, GRID_W, NA_W), lambda b, j: (b, j, C_ZN // NA_W)),
                  pl.BlockSpec((None, None, NA_HEADS, GRID_W, NA_ROWS * GRID_W),
                               lambda b, j: (layer, _na_dr(j), 0, 0, 0))],
        out_specs=pl.BlockSpec((None, GRID_W, NA_W), lambda b, j: (b, j, 0)),
        out_shape=jax.ShapeDtypeStruct((bsz, t, NA_W), F32),
        compiler_params=pltpu.CompilerParams(vmem_limit_bytes=VMEM_BIG),
        name="natten",
    )(p, p, p, p, bias_tab)


def _swap16(x):
    lane = lax.broadcasted_iota(jnp.int32, x.shape, 1)
    return jnp.where(lane % 32 < 16, pltpu.roll(x, LANE - 16, 1), pltpu.roll(x, 16, 1))


def _gdn_prep_kernel(q_ref, k_ref, v_ref, cw_ref, cos_ref, sin_ref, hsum_ref, qo_ref, ko_ref, vo_ref, pad_ref):
    t = q_ref.shape[0]
    hsum = hsum_ref[...]

    def one(src_ref, dst_ref, cw, mode):
        _fill_padded(src_ref, pad_ref, t)

        def body(c, carry):
            t0 = pl.multiple_of(c * RC, RC)
            x = _silu(_conv_chunk(pad_ref, cw, t0))
            if mode != "v":
                ssq = _dot(x * x, hsum, HIGHEST)
                x = x * lax.rsqrt(ssq + NORM_EPS)
            dst_ref[pl.ds(t0, RC), :] = x
            return carry

        lax.fori_loop(0, t // RC, body, 0)
        if mode == "v":
            return

        def rope(c, carry):
            t0 = pl.multiple_of(LC + c * RC, RC)
            r0 = pl.multiple_of(c * RC, RC)
            x = dst_ref[pl.ds(t0, RC), :]
            x = x * cos_ref[pl.ds(r0, RC), :] + _swap16(x) * sin_ref[pl.ds(r0, RC), :]
            dst_ref[pl.ds(t0, RC), :] = x
            return carry

        lax.fori_loop(0, (t - LC) // RC, rope, 0)
        if mode == "q":
            def scale(c, carry):
                t0 = pl.multiple_of(c * RC, RC)
                dst_ref[pl.ds(t0, RC), :] = dst_ref[pl.ds(t0, RC), :] * (HD ** -0.5)
                return carry

            lax.fori_loop(0, t // RC, scale, 0)

    one(q_ref, qo_ref, cw_ref[0], "q")
    one(k_ref, ko_ref, cw_ref[1], "k")
    one(v_ref, vo_ref, cw_ref[2], "v")


def _gdn_prep(p, conv_w3, cos_t, sin_t, hsum, layer):
    bsz, t, _ = p.shape
    ns = GDN_W // LANE
    col = lambda c0: (lambda b, s: (b, 0, c0 // LANE + s))
    out = jax.ShapeDtypeStruct((bsz, t, GDN_W), F32)
    return pl.pallas_call(
        _gdn_prep_kernel,
        grid=(bsz, ns),
        in_specs=[pl.BlockSpec((None, t, LANE), col(C_GQ)),
                  pl.BlockSpec((None, t, LANE), col(C_GK)),
                  pl.BlockSpec((None, t, LANE), col(C_GV)),
                  pl.BlockSpec((None, 3, CONV_K, LANE), lambda b, s: (layer, 0, 0, s)),
                  pl.BlockSpec((t - LC, LANE), lambda b, s: (0, 0)),
                  pl.BlockSpec((t - LC, LANE), lambda b, s: (0, 0)),
                  pl.BlockSpec((LANE, LANE), lambda b, s: (0, 0))],
        out_specs=[pl.BlockSpec((None, t, LANE), lambda b, s: (b, 0, s))] * 3,
        out_shape=[out, out, out],
        scratch_shapes=[pltpu.VMEM((t + PAD_ROWS, LANE), F32)],
        compiler_params=pltpu.CompilerParams(vmem_limit_bytes=VMEM_BIG),
        name="gdn_prep",
    )(p, p, p, conv_w3, cos_t, sin_t, hsum)


GB = 256
NCH = GB // CHUNK


def _gdn_scan_kernel(qf_ref, kf_ref, vf_ref, gf_ref, qb_ref, kb_ref, vb_ref, gb_ref,
                     esel_ref, gvec_ref, of_ref, ob_ref, s_ref):
    step = pl.program_id(1)
    w4 = GDN_W

    @pl.when(step == 0)
    def _():
        s_ref[...] = jnp.zeros_like(s_ref)

    row = lax.broadcasted_iota(jnp.int32, (CHUNK, w4), 0)
    jl = lax.broadcasted_iota(jnp.int32, (CHUNK, w4), 1) % HD
    r2 = lax.broadcasted_iota(jnp.int32, (w4, w4), 0)
    c2 = lax.broadcasted_iota(jnp.int32, (w4, w4), 1)
    bd = (r2 // HD) == (c2 // HD)
    eye = (jl == row).astype(F32)
    blk16 = (row // 16) == (jl // 16)
    blk32 = (row // 32) == (jl // 32)
    alog = gvec_ref[0:1, :]
    dtb = gvec_ref[1:2, :]
    glane = lax.broadcasted_iota(jnp.int32, (GB, LANE), 1)

    def expand(y):
        yb = y.astype(BF16)
        return jnp.where(bd, jnp.concatenate([yb, yb, yb, yb], axis=0), jnp.zeros((), BF16))

    def mm(x, y):
        return _dot(x.astype(BF16), expand(y))

    refs = ((qf_ref, kf_ref, vf_ref, gf_ref), (qb_ref, kb_ref, vb_ref, gb_ref))
    chains = []
    for d in range(2):
        q_ref, k_ref, v_ref, g_ref = refs[d]
        raw = g_ref[...]
        comp = jnp.where(glane < 2 * GDN_HEADS, jax.nn.sigmoid(raw), -jnp.exp(alog) * _softplus(raw + dtb))
        gexp = _dot(comp, esel_ref[d], HIGHEST)
        if d == 0:
            incl, strict, upto, tri = jl <= row, jl < row, row <= jl, bd & (c2 <= r2)
        else:
            incl, strict, upto, tri = jl >= row, jl > row, row >= jl, bd & (c2 >= r2)
        last = CHUNK - 1 if d == 0 else 0
        g_all = gexp[:, w4:2 * w4]
        gcum_all = _dot(tri.astype(F32), g_all, HIGHEST)
        for c in range(NCH):
            rows = slice(c * CHUNK, (c + 1) * CHUNK)
            ch = dict(d=d, c=c, q=q_ref[rows, :], k=k_ref[rows, :], v=v_ref[rows, :], beta=gexp[rows, 0:w4],
                      gcum=gcum_all[rows], incl=incl, strict=strict, last=last)
            grow = jnp.sum(jnp.where(upto, g_all[rows], 0.0), axis=0, keepdims=True)
            ch["decay"] = jnp.where(incl, jnp.exp(jnp.minimum(ch["gcum"] - grow, 0.0)), 0.0)
            ch["kbeta"] = ch["k"] * ch["beta"]
            chains.append(ch)

    for ch in chains:
        prod = _dot_nt(jnp.concatenate([ch["kbeta"], ch["q"]], axis=0).astype(BF16), expand(ch["k"]))
        ch["n"] = jnp.where(ch["strict"], prod[0:CHUNK] * ch["decay"], 0.0)
        ch["qk"] = jnp.where(ch["incl"], prod[CHUNK:2 * CHUNK] * ch["decay"], 0.0)
        n16 = jnp.where(blk16, ch["n"], 0.0)
        ch["t"] = eye - n16
        ch["pw"] = n16
    for _ in range(3):
        for ch in chains:
            ch["pw"] = mm(ch["pw"], ch["pw"])
        for ch in chains:
            ch["t"] = ch["t"] + mm(ch["t"], ch["pw"])
    for off in (jnp.where(blk32 & ~blk16, 1.0, 0.0), jnp.where(blk32, 0.0, 1.0)):
        for ch in chains:
            ch["te"] = mm(ch["t"], ch["n"] * off)
        for ch in chains:
            ch["t"] = ch["t"] - mm(ch["te"], ch["t"])
    for ch in chains:
        eg = jnp.exp(ch["gcum"])
        ch["u"] = mm(ch["t"], ch["v"] * ch["beta"])
        w = mm(ch["t"], ch["kbeta"] * eg)
        glast = ch["gcum"][ch["last"]:ch["last"] + 1, :]
        ch["wq"] = jnp.concatenate([w, ch["q"] * eg], axis=0).astype(BF16)
        ch["kg"] = (ch["k"] * jnp.exp(glast - ch["gcum"])).astype(BF16)
        ch["eglast"] = jnp.exp(glast)

    state = [s_ref[0], s_ref[1]]
    res = {}
    for i in range(NCH):
        cur = [chains[i], chains[NCH + NCH - 1 - i]]
        ws = [_dot(ch["wq"], state[d].astype(BF16)) for d, ch in enumerate(cur)]
        v_new = [ch["u"] - ws[d][0:CHUNK] for d, ch in enumerate(cur)]
        for d, ch in enumerate(cur):
            res[(d, ch["c"])] = ws[d][CHUNK:2 * CHUNK] + mm(ch["qk"], v_new[d])
        for d, ch in enumerate(cur):
            upd = _dot_tn(ch["kg"], v_new[d].astype(BF16))
            state[d] = state[d] * ch["eglast"] + jnp.where(bd, upd, 0.0)
    outs = (of_ref, ob_ref)
    for d in range(2):
        s_ref[d] = state[d]
        for c in range(NCH):
            outs[d][c * CHUNK:(c + 1) * CHUNK, :] = res[(d, c)]


def _gdn_scan(qn, kn, vn, p, esel, gvec, layer):
    bsz, t, _ = p.shape
    nsteps = t // GB
    fwd = lambda b, i: (b, i, 0)
    bwd = lambda b, i: (b, jnp.where(i == 0, 0, nsteps - i), 0)
    gcol = C_GATE // LANE
    fwd_g = lambda b, i: (b, i, gcol)
    bwd_g = lambda b, i: (b, jnp.where(i == 0, 0, nsteps - i), gcol)
    blk = lambda im: pl.BlockSpec((None, GB, GDN_W), im)
    out = jax.ShapeDtypeStruct((bsz, t, GDN_W), F32)
    return pl.pallas_call(
        _gdn_scan_kernel,
        grid=(bsz, nsteps),
        in_specs=[blk(fwd), blk(fwd), blk(fwd), pl.BlockSpec((None, GB, LANE), fwd_g),
                  blk(bwd), blk(bwd), blk(bwd), pl.BlockSpec((None, GB, LANE), bwd_g),
                  pl.BlockSpec((2, LANE, 2 * GDN_W), lambda b, i: (0, 0, 0)),
                  pl.BlockSpec((None, 2, LANE), lambda b, i: (layer, 0, 0))],
        out_specs=[blk(fwd), blk(bwd)],
        out_shape=[out, out],
        scratch_shapes=[pltpu.VMEM((2, GDN_W, GDN_W), F32)],
        compiler_params=pltpu.CompilerParams(vmem_limit_bytes=VMEM_BIG),
        name="gdn_scan",
    )(qn, kn, vn, p, qn, kn, vn, p, esel, gvec)


def _outproj_kernel(x_ref, h_ref, nb_ref, of_ref, ob_ref, zg_ref, m_ref, w_ref, nw_ref, hsum_ref,
                    lng_ref, lnb_ref, o_ref):
    o = of_ref[...] + ob_ref[...]
    ms = _dot(o * o, hsum_ref[...], HIGHEST) * (1.0 / HD)
    og = o * lax.rsqrt(ms + NORM_EPS) * nw_ref[...] * _silu(zg_ref[...])
    y = (_dot(h_ref[...].astype(BF16), w_ref[0:RG_W, :])
         + _dot(nb_ref[...].astype(BF16), w_ref[RG_W:RG_W + NA_W, :])
         + _dot(og.astype(BF16), w_ref[RG_W + NA_W:D, :]))
    gate = m_ref[...][:, 2 * D:3 * D]
    z = DEEPNORM_ALPHA * x_ref[...] + gate * y
    mu = jnp.mean(z, axis=-1, keepdims=True)
    zc = z - mu
    var = jnp.mean(zc * zc, axis=-1, keepdims=True)
    o_ref[...] = zc * lax.rsqrt(var + LN_EPS) * lng_ref[...] + lnb_ref[...]


def _out_proj(xa, hg, nb, o_f, o_b, p, modsel, w_out_b, nw4, hsum4, ln_g, ln_b, layer):
    bsz, t, _ = xa.shape
    nt = t // TM
    row = lambda width: pl.BlockSpec((None, TM, width), lambda b, i: (b, i, 0))
    vec = lambda: pl.BlockSpec((None, 1, D), lambda b, i: (layer, 0, 0))
    return pl.pallas_call(
        _outproj_kernel,
        grid=(bsz, nt),
        in_specs=[row(D), row(RG_W), row(NA_W), row(GDN_W), row(GDN_W),
                  pl.BlockSpec((None, TM, GDN_W), lambda b, i: (b, i, C_ZG // GDN_W)),
                  pl.BlockSpec((None, None, 1, 3 * D), lambda b, i: (layer, 2 * b + jnp.minimum(i, 1), 0, 0)),
                  pl.BlockSpec((None, D, D), lambda b, i: (layer, 0, 0)),
                  pl.BlockSpec((None, 1, GDN_W), lambda b, i: (layer, 0, 0)),
                  pl.BlockSpec((GDN_W, GDN_W), lambda b, i: (0, 0)),
                  vec(), vec()],
        out_specs=row(D),
        out_shape=jax.ShapeDtypeStruct((bsz, t, D), F32),
        compiler_params=pltpu.CompilerParams(vmem_limit_bytes=VMEM_BIG),
        name="out_proj",
    )(xa, hg, nb, o_f, o_b, p, modsel, w_out_b, nw4, hsum4, ln_g, ln_b)


def _na_bias_table(na_rpb):
    jq = np.arange(GRID_W)
    col_start = np.clip(jq - NA_COLS // 2, 0, GRID_W - NA_COLS)
    kc = np.arange(GRID_W)
    inside = (kc[None, :] >= col_start[:, None]) & (kc[None, :] < col_start[:, None] + NA_COLS)
    col_off = np.clip(kc[None, :] - jq[:, None] + NA_COLS - 1, 0, 2 * NA_COLS - 2)
    dr = np.arange(NA_ROWS)
    wr = np.arange(NA_ROWS)
    row_off = wr[None, :] - dr[:, None] + NA_ROWS - 1
    tab = na_rpb[:, :, row_off][:, :, :, :, col_off]
    tab = jnp.where(inside[None, None, None, None], tab, NEG)
    tab = jnp.transpose(tab, (0, 2, 1, 4, 3, 5))
    return tab.reshape(DEPTH, NA_ROWS, NA_HEADS, GRID_W, NA_ROWS * GRID_W)


def _rope_tables(seq):
    pos = jnp.arange(seq)
    rows_pos, cols_pos = pos // GRID_W, pos % GRID_W
    half = HD // 2
    nf = half // 2
    inv_freq = ROPE_BASE ** (-jnp.arange(nf, dtype=F32) / nf)
    lane = np.arange(LANE)
    jl = lane % HD
    use_row = jl < half
    f = (jl % half) % nf
    sign = np.where((jl % half) < nf, -1.0, 1.0).astype(np.float32)
    pos_l = jnp.where(use_row[None, :], rows_pos[:, None], cols_pos[:, None]).astype(F32)
    ang = pos_l * inv_freq[f][None, :]
    return jnp.cos(ang), jnp.sin(ang) * sign[None, :]


def _gate_select():
    e = np.zeros((2, LANE, 2 * GDN_W), np.float32)
    for d in range(2):
        for l in range(GDN_W):
            h = l // HD
            e[d, d * GDN_HEADS + h, l] = 1.0
            e[d, 2 * GDN_HEADS + d * GDN_HEADS + h, GDN_W + l] = 1.0
    return jnp.asarray(e)


def _head_sum(width):
    i = np.arange(width)
    return jnp.asarray((i[:, None] // HD == i[None, :] // HD).astype(np.float32))


def kernel(x, c, ctx, c_ctx, w_mod, b_mod, w_in, conv_w, rg_wa, rg_ba, rg_wx, rg_bx, rg_lam, na_rpb, gdn_alog,
           gdn_dtb, gdn_nw, w_out, ln_g, ln_b):
    bsz, seq, _ = x.shape
    assert ctx.shape[1] == LC and seq == GRID_W * GRID_W

    c8 = jnp.concatenate([c, c_ctx[None], jnp.zeros((SUB - bsz - 1, D), F32)], axis=0)
    w_in_p = jnp.pad(w_in, ((0, 0), (0, 0), (0, D_INP - D_IN))).astype(BF16)
    w_out_b = w_out.astype(BF16)
    ns = RG_W // LANE
    eye2 = jnp.eye(2, dtype=F32)

    def block_diag(wt):
        wt = wt.reshape(DEPTH, 2, ns, 2, RG_BLOCK, RG_BLOCK)
        return jnp.einsum('ldsjae,jk->ldsjake', wt, eye2).reshape(DEPTH, 2, ns, LANE, LANE)

    wa, wx = block_diag(rg_wa), block_diag(rg_wx)
    rg_w = jnp.concatenate([wa[:, 0], wx[:, 0], wa[:, 1], wx[:, 1]], axis=-1)
    slab = lambda v: v.reshape(DEPTH, ns, 1, LANE)
    rg_b = jnp.concatenate([slab(rg_ba[:, 0]), slab(rg_bx[:, 0]), slab(rg_ba[:, 1]), slab(rg_bx[:, 1])], axis=-1)
    rg_lam2 = jnp.concatenate([slab(rg_lam[:, 0]), slab(rg_lam[:, 1])], axis=-1)
    conv_w3 = jnp.stack([conv_w[:, :, C_GQ:C_GK], conv_w[:, :, C_GK:C_GV], conv_w[:, :, C_GV:C_ZA]], axis=1)
    bias_tab = _na_bias_table(na_rpb)
    cos_t, sin_t = _rope_tables(seq)
    esel = _gate_select()
    pad = jnp.zeros((DEPTH, LANE - 4 * GDN_HEADS), F32)
    zero8 = jnp.zeros((DEPTH, 2 * GDN_HEADS), F32)
    gvec = jnp.stack([jnp.concatenate([zero8, gdn_alog.reshape(DEPTH, -1), pad], axis=-1),
                      jnp.concatenate([zero8, gdn_dtb.reshape(DEPTH, -1), pad], axis=-1)], axis=1)
    nw4 = jnp.tile(gdn_nw, (1, GDN_HEADS)).reshape(DEPTH, 1, GDN_W)
    hsum2, hsum4 = _head_sum(LANE), _head_sum(GDN_W)
    ln_g3, ln_b3 = ln_g.reshape(DEPTH, 1, D), ln_b.reshape(DEPTH, 1, D)

    mods = _modulation(c8, w_mod, b_mod)
    ctx_rows = jnp.broadcast_to(mods[:, bsz:bsz + 1], (DEPTH, bsz, 3 * D))
    modsel = jnp.stack([ctx_rows, mods[:, :bsz]], axis=2).reshape(DEPTH, 2 * bsz, 1, 3 * D)

    xa = jnp.concatenate([ctx, x], axis=1)
    for layer in range(DEPTH):
        p = _in_proj(xa, modsel, w_in_p, layer)
        hg = _rglru(p, conv_w, rg_w, rg_b, rg_lam2, layer)
        nb = _natten(p, bias_tab, layer)
        qn, kn, vn = _gdn_prep(p, conv_w3, cos_t, sin_t, hsum2, layer)
        o_f, o_b = _gdn_scan(qn, kn, vn, p, esel, gvec, layer)
        xa = _out_proj(xa, hg, nb, o_f, o_b, p, modsel, w_out_b, nw4, hsum4, ln_g3, ln_b3, layer)
    return xa[:, LC:]
```

```python
import functools

import numpy as np
import jax
import jax.numpy as jnp
from jax import lax
from jax.experimental import pallas as pl
from jax.experimental.pallas import tpu as pltpu

F32 = jnp.float32
BF16 = jnp.bfloat16
HIGHEST = lax.Precision.HIGHEST

D = 1024
DEPTH = 4
LC = 256
GRID_W = 64
CONV_K = 4
RG_W = 384
RG_BLOCK = 64
RG_C = 8.0
NA_HEADS = 6
NA_W = 384
NA_ROWS = 8
NA_COLS = 16
GDN_HEADS = 4
GDN_W = 256
HD = 64
CHUNK = 64
ROPE_BASE = 10000.0
DEEPNORM_ALPHA = (2.0 * DEPTH) ** 0.25
LN_EPS = 1e-5
NORM_EPS = 1e-6
NEG = -1e30

C_RGX = 0
C_GQ = 384
C_GK = 640
C_GV = 896
C_ZA = 1152
C_NQ = 1536
C_NK = 1920
C_NV = 2304
C_ZN = 2688
C_ZG = 3072
C_GATE = 3328
D_IN = 3344
D_INP = 3456

LANE = 128
SUB = 8
TM = 256
RC = 128
VMEM_BIG = 56 * 1024 * 1024


def _sigmoid(x):
    return 0.5 * jnp.tanh(0.5 * x) + 0.5


def _silu(x):
    return x * _sigmoid(x)


def _softplus(x):
    return jnp.maximum(x, 0.0) + jnp.log1p(jnp.exp(-jnp.abs(x)))


def _dot(a, b, precision=None):
    return jnp.dot(a, b, preferred_element_type=F32, precision=precision)


def _dot_nt(a, b, precision=None):
    return lax.dot_general(a, b, (((1,), (1,)), ((), ())), preferred_element_type=F32, precision=precision)


def _dot_tn(a, b, precision=None):
    return lax.dot_general(a, b, (((0,), (0,)), ((), ())), preferred_element_type=F32, precision=precision)


def _mod_kernel(c_ref, w_ref, b_ref, o_ref):
    s = _silu(c_ref[...])
    o_ref[...] = _dot(s, w_ref[...], HIGHEST) + b_ref[...]


def _modulation(c8, w_mod, b_mod):
    nb = 3 * D // D
    return pl.pallas_call(
        _mod_kernel,
        grid=(DEPTH, nb),
        in_specs=[pl.BlockSpec((SUB, D), lambda l, j: (0, 0)),
                  pl.BlockSpec((None, D, D), lambda l, j: (l, 0, j)),
                  pl.BlockSpec((None, 1, D), lambda l, j: (l, 0, j))],
        out_specs=pl.BlockSpec((None, SUB, D), lambda l, j: (l, 0, j)),
        out_shape=jax.ShapeDtypeStruct((DEPTH, SUB, 3 * D), F32),
        name="modulation",
    )(c8, w_mod, b_mod.reshape(DEPTH, 1, 3 * D))


def _inproj_kernel(x_ref, m_ref, w_ref, o_ref, kv_ref):
    m = m_ref[...]
    shift = m[:, :D]
    scale = m[:, D:2 * D]
    u = x_ref[...] * (1.0 + scale) + shift
    p = _dot(u.astype(BF16), w_ref[...])
    o_ref[...] = p
    kv_ref[...] = p[:, C_NK:C_ZN].astype(BF16)


def _in_proj(xa, modsel, w_in_p, layer):
    bsz, t, _ = xa.shape
    nt = t // TM
    return pl.pallas_call(
        _inproj_kernel,
        grid=(bsz, nt),
        in_specs=[pl.BlockSpec((None, TM, D), lambda b, i: (b, i, 0)),
                  pl.BlockSpec((None, None, 1, 3 * D), lambda b, i: (layer, 2 * b + jnp.minimum(i, 1), 0, 0)),
                  pl.BlockSpec((None, D, D_INP), lambda b, i: (layer, 0, 0))],
        out_specs=[pl.BlockSpec((None, TM, D_INP), lambda b, i: (b, i, 0)),
                   pl.BlockSpec((None, TM, 2 * NA_W), lambda b, i: (b, i, 0))],
        out_shape=[jax.ShapeDtypeStruct((bsz, t, D_INP), F32), jax.ShapeDtypeStruct((bsz, t, 2 * NA_W), BF16)],
        compiler_params=pltpu.CompilerParams(vmem_limit_bytes=VMEM_BIG),
        name="in_proj",
    )(xa, modsel, w_in_p)


PAD_ROWS = 3 * SUB


def _fill_padded(src_ref, pad_ref, t):
    zeros = jnp.zeros((SUB, LANE), F32)
    pad_ref[0:SUB, :] = zeros
    pad_ref[SUB + LC:2 * SUB + LC, :] = zeros
    pad_ref[t + 2 * SUB:t + 3 * SUB, :] = zeros

    def body(c, carry):
        src = pl.multiple_of(c * LC, LC)
        dst = pl.multiple_of(src + SUB + jnp.where(c >= 1, SUB, 0), SUB)
        pad_ref[pl.ds(dst, LC), :] = src_ref[pl.ds(src, LC), :]
        return carry

    lax.fori_loop(0, t // LC, body, 0)


def _conv_chunk(pad_ref, cw, t0):
    start = pl.multiple_of(t0 + jnp.where(t0 >= LC, SUB, 0), SUB)
    win = pad_ref[pl.ds(start, RC + 2 * SUB), :]
    n = RC + 2 * SUB
    acc = win[SUB:SUB + RC] * cw[2:3, :]
    for k, shift in ((0, 2), (1, 1), (3, n - 1)):
        acc = acc + pltpu.roll(win, shift, 0)[SUB:SUB + RC] * cw[k:k + 1, :]
    return acc


def _scan8(a, b, row, reverse):
    for s in (1, 2, 4):
        if reverse:
            sh, ok = SUB - s, row < SUB - s
        else:
            sh, ok = s, row >= s
        a_sh = jnp.where(ok, pltpu.roll(a, sh, 0), 1.0)
        b_sh = jnp.where(ok, pltpu.roll(b, sh, 0), 0.0)
        b = a * b_sh + b
        a = a * a_sh
    return a, b


def _rglru_kernel(x_ref, z_ref, cw_ref, w_ref, bias_ref, lam_ref, o_ref,
                  pad_ref, a0_ref, b0_ref, a1_ref, b1_ref, hf_ref, hb_ref):
    t = x_ref.shape[0]
    _fill_padded(x_ref, pad_ref, t)
    cw = cw_ref[...]
    w = w_ref[...]
    bias = bias_ref[...]
    ls = -_softplus(-lam_ref[...])

    def gates(c, carry):
        t0 = pl.multiple_of(c * RC, RC)
        u = _conv_chunk(pad_ref, cw, t0)
        g = _dot(u.astype(BF16), w) + bias
        for d, (a_ref, b_ref) in enumerate(((a0_ref, b0_ref), (a1_ref, b1_ref))):
            r = _sigmoid(g[:, (2 * d) * LANE:(2 * d + 1) * LANE])
            i = _sigmoid(g[:, (2 * d + 1) * LANE:(2 * d + 2) * LANE])
            log_a = RG_C * r * ls[:, d * LANE:(d + 1) * LANE]
            a = jnp.exp(log_a)
            mult = jnp.sqrt(-jnp.tanh(log_a) * (a * a + 1.0))
            a_ref[pl.ds(t0, RC), :] = a
            b_ref[pl.ds(t0, RC), :] = mult * (i * u)
        return carry

    lax.fori_loop(0, t // RC, gates, 0, unroll=2)

    row = lax.broadcasted_iota(jnp.int32, (SUB, LANE), 0)
    n_tiles = t // SUB
    n_ctx = LC // SUB

    def scan(i, carry):
        cf, cb = carry
        rf = pl.multiple_of(i * SUB, SUB)
        af, bf = _scan8(a0_ref[pl.ds(rf, SUB), :], b0_ref[pl.ds(rf, SUB), :], row, False)
        hf = bf + af * cf
        hf_ref[pl.ds(rf, SUB), :] = hf
        cf = jnp.broadcast_to(hf[SUB - 1:SUB, :], (SUB, LANE))
        j = jnp.where(i < n_ctx, n_ctx - 1 - i, n_tiles + n_ctx - 1 - i)
        rb = pl.multiple_of(j * SUB, SUB)
        ab, bb = _scan8(a1_ref[pl.ds(rb, SUB), :], b1_ref[pl.ds(rb, SUB), :], row, True)
        hb = bb + ab * cb
        hb_ref[pl.ds(rb, SUB), :] = hb
        cb = jnp.broadcast_to(hb[0:1, :], (SUB, LANE))
        return cf, cb

    zero = jnp.zeros((SUB, LANE), F32)
    lax.fori_loop(0, n_tiles, scan, (zero, zero), unroll=2)

    def finish(c, carry):
        t0 = pl.multiple_of(c * RC, RC)
        h = hf_ref[pl.ds(t0, RC), :] + hb_ref[pl.ds(t0, RC), :]
        o_ref[pl.ds(t0, RC), :] = h * _silu(z_ref[pl.ds(t0, RC), :])
        return carry

    lax.fori_loop(0, t // RC, finish, 0)


def _rglru(p, conv_w, rg_w, rg_b, rg_lam2, layer):
    bsz, t, _ = p.shape
    ns = RG_W // LANE
    seq = lambda: pltpu.VMEM((t, LANE), F32)
    return pl.pallas_call(
        _rglru_kernel,
        grid=(bsz, ns),
        in_specs=[pl.BlockSpec((None, t, LANE), lambda b, s: (b, 0, C_RGX // LANE + s)),
                  pl.BlockSpec((None, t, LANE), lambda b, s: (b, 0, C_ZA // LANE + s)),
                  pl.BlockSpec((None, CONV_K, LANE), lambda b, s: (layer, 0, C_RGX // LANE + s)),
                  pl.BlockSpec((None, None, LANE, 4 * LANE), lambda b, s: (layer, s, 0, 0)),
                  pl.BlockSpec((None, None, 1, 4 * LANE), lambda b, s: (layer, s, 0, 0)),
                  pl.BlockSpec((None, None, 1, 2 * LANE), lambda b, s: (layer, s, 0, 0))],
        out_specs=pl.BlockSpec((None, t, LANE), lambda b, s: (b, 0, s)),
        out_shape=jax.ShapeDtypeStruct((bsz, t, RG_W), F32),
        scratch_shapes=[pltpu.VMEM((t + PAD_ROWS, LANE), F32), seq(), seq(), seq(), seq(), seq(), seq()],
        compiler_params=pltpu.CompilerParams(vmem_limit_bytes=VMEM_BIG),
        name="rglru",
    )(p, p, conv_w, rg_w, rg_b, rg_lam2)


NA_QR = 2
NA_NQ = NA_QR * GRID_W
NA_KR = 10
NA_MASKED = 2 * NA_ROWS - 1


def _na_kernel(q_ref, kv_ref, z_ref, tab_ref, o_ref):
    j = pl.program_id(1)
    n_ctx_steps = LC // NA_NQ
    ns = NA_W // LANE
    lane = lax.broadcasted_iota(jnp.int32, (NA_NQ, LANE), 1)
    first = lane < HD
    first_blk = lax.broadcasted_iota(jnp.int32, (GRID_W, LANE), 1) < HD
    q = q_ref[...] * (HD ** -0.5)

    def stacked_q(s):
        qs = q[:, s * LANE:(s + 1) * LANE]
        return jnp.concatenate([jnp.where(first, qs, 0.0), jnp.where(first, 0.0, qs)], axis=0).astype(BF16)

    def finish(s, sc, vv, n_keys):
        m = jnp.max(sc, axis=-1, keepdims=True)
        pr = jnp.exp(sc - m)
        den = jnp.sum(pr, axis=-1, keepdims=True)
        pv = _dot(pr.astype(BF16), vv) * (1.0 / den)
        o = jnp.where(first, pv[0:NA_NQ], pv[NA_NQ:2 * NA_NQ])
        cols = slice(s * LANE, (s + 1) * LANE)
        o_ref[:, cols] = o * _silu(z_ref[:, cols])

    @pl.when(j >= n_ctx_steps)
    def _():
        r = NA_QR * (j - n_ctx_steps)
        r0 = [jnp.clip(r + i - NA_ROWS // 2, 0, GRID_W - NA_ROWS) for i in range(NA_QR)]
        start = jnp.minimum(r0[0], GRID_W - NA_KR)
        base = pl.multiple_of(LC + start * GRID_W, GRID_W)
        idx = [[jnp.where((start + wu >= r0[i]) & (start + wu < r0[i] + NA_ROWS),
                          start + wu - (r + i) + NA_ROWS - 1, NA_MASKED) for wu in range(NA_KR)]
               for i in range(NA_QR)]
        n_loc = NA_KR * GRID_W
        scores = []
        for s in range(ns):
            kk = jnp.concatenate([kv_ref[pl.ds(base, n_loc), s * LANE:(s + 1) * LANE],
                                  kv_ref[0:LC, s * LANE:(s + 1) * LANE]], axis=0)
            scores.append(_dot_nt(stacked_q(s), kk))
        for s in range(ns):
            rows = []
            for hh in range(2):
                h = 2 * s + hh
                for i in range(NA_QR):
                    rows.append(jnp.concatenate(
                        [jnp.where(first_blk, tab_ref[h, idx[i][2 * a]], tab_ref[h, idx[i][2 * a + 1]])
                         for a in range(NA_KR // 2)] + [jnp.zeros((GRID_W, LC), F32)], axis=1))
            scores[s] = scores[s] + jnp.concatenate(rows, axis=0)
        for s in range(ns):
            vv = jnp.concatenate([kv_ref[pl.ds(base, n_loc), NA_W + s * LANE:NA_W + (s + 1) * LANE],
                                  kv_ref[0:LC, NA_W + s * LANE:NA_W + (s + 1) * LANE]], axis=0)
            finish(s, scores[s], vv, n_loc + LC)

    @pl.when(j < n_ctx_steps)
    def _():
        scores = [_dot_nt(stacked_q(s), kv_ref[0:LC, s * LANE:(s + 1) * LANE]) for s in range(ns)]
        for s in range(ns):
            finish(s, scores[s], kv_ref[0:LC, NA_W + s * LANE:NA_W + (s + 1) * LANE], LC)


def _natten(p, kv16, bias_tab, layer):
    bsz, t, _ = p.shape
    nq = t // NA_NQ
    return pl.pallas_call(
        _na_kernel,
        grid=(bsz, nq),
        in_specs=[pl.BlockSpec((None, NA_NQ, NA_W), lambda b, j: (b, j, C_NQ // NA_W)),
                  pl.BlockSpec((None, t, 2 * NA_W), lambda b, j: (b, 0, 0)),
                  pl.BlockSpec((None, NA_NQ, NA_W), lambda b, j: (b, j, C_ZN // NA_W)),
                  pl.BlockSpec((None, NA_HEADS, 2 * NA_ROWS, GRID_W, LANE), lambda b, j: (layer, 0, 0, 0, 0))],
        out_specs=pl.BlockSpec((None, NA_NQ, NA_W), lambda b, j: (b, j, 0)),
        out_shape=jax.ShapeDtypeStruct((bsz, t, NA_W), F32),
        compiler_params=pltpu.CompilerParams(vmem_limit_bytes=VMEM_BIG),
        name="natten",
    )(p, kv16, p, bias_tab)


def _swap16(x):
    lane = lax.broadcasted_iota(jnp.int32, x.shape, 1)
    return jnp.where(lane % 32 < 16, pltpu.roll(x, LANE - 16, 1), pltpu.roll(x, 16, 1))


def _head_sum_sq(x, hsum):
    x2 = x * x
    hi = x2.astype(BF16)
    lo = (x2 - hi.astype(F32)).astype(BF16)
    return _dot(hi, hsum) + _dot(lo, hsum)


def _gdn_prep_kernel(q_ref, k_ref, v_ref, cw_ref, cos_ref, sin_ref, hsum_ref, qo_ref, ko_ref, vo_ref, pad_ref):
    t = q_ref.shape[0]
    hsum = hsum_ref[...]

    def one(src_ref, dst_ref, cw, mode):
        _fill_padded(src_ref, pad_ref, t)

        def body(c, carry):
            t0 = pl.multiple_of(c * RC, RC)
            x = _silu(_conv_chunk(pad_ref, cw, t0))
            if mode != "v":
                x = x * lax.rsqrt(_head_sum_sq(x, hsum) + NORM_EPS)
                x = x * cos_ref[pl.ds(t0, RC), :] + _swap16(x) * sin_ref[pl.ds(t0, RC), :]
            if mode == "q":
                x = x * (HD ** -0.5)
            dst_ref[pl.ds(t0, RC), :] = x
            return carry

        lax.fori_loop(0, t // RC, body, 0, unroll=2)

    one(q_ref, qo_ref, cw_ref[0], "q")
    one(k_ref, ko_ref, cw_ref[1], "k")
    one(v_ref, vo_ref, cw_ref[2], "v")


def _gdn_prep(p, conv_w3, cos_t, sin_t, hsum, layer):
    bsz, t, _ = p.shape
    ns = GDN_W // LANE
    col = lambda c0: (lambda b, s: (b, 0, c0 // LANE + s))
    out = jax.ShapeDtypeStruct((bsz, t, GDN_W), F32)
    return pl.pallas_call(
        _gdn_prep_kernel,
        grid=(bsz, ns),
        in_specs=[pl.BlockSpec((None, t, LANE), col(C_GQ)),
                  pl.BlockSpec((None, t, LANE), col(C_GK)),
                  pl.BlockSpec((None, t, LANE), col(C_GV)),
                  pl.BlockSpec((None, 3, CONV_K, LANE), lambda b, s: (layer, 0, 0, s)),
                  pl.BlockSpec((t, LANE), lambda b, s: (0, 0)),
                  pl.BlockSpec((t, LANE), lambda b, s: (0, 0)),
                  pl.BlockSpec((LANE, LANE), lambda b, s: (0, 0))],
        out_specs=[pl.BlockSpec((None, t, LANE), lambda b, s: (b, 0, s))] * 3,
        out_shape=[out, out, out],
        scratch_shapes=[pltpu.VMEM((t + PAD_ROWS, LANE), F32)],
        compiler_params=pltpu.CompilerParams(vmem_limit_bytes=VMEM_BIG),
        name="gdn_prep",
    )(p, p, p, conv_w3, cos_t, sin_t, hsum)


GB = 256
NCH = GB // CHUNK


def _gdn_scan_kernel(qf_ref, kf_ref, vf_ref, gf_ref, qb_ref, kb_ref, vb_ref, gb_ref,
                     esel_ref, gvec_ref, of_ref, ob_ref, s_ref):
    step = pl.program_id(1)
    w4 = GDN_W

    @pl.when(step == 0)
    def _():
        s_ref[...] = jnp.zeros_like(s_ref)

    row = lax.broadcasted_iota(jnp.int32, (CHUNK, w4), 0)
    jl = lax.broadcasted_iota(jnp.int32, (CHUNK, w4), 1) % HD
    r2 = lax.broadcasted_iota(jnp.int32, (w4, w4), 0)
    c2 = lax.broadcasted_iota(jnp.int32, (w4, w4), 1)
    bd = (r2 // HD) == (c2 // HD)
    eye = (jl == row).astype(F32)
    blk16 = (row // 16) == (jl // 16)
    blk32 = (row // 32) == (jl // 32)
    alog = gvec_ref[0:1, :]
    dtb = gvec_ref[1:2, :]
    glane = lax.broadcasted_iota(jnp.int32, (GB, LANE), 1)

    def expand(y):
        yb = y.astype(BF16)
        return jnp.where(bd, jnp.concatenate([yb, yb, yb, yb], axis=0), jnp.zeros((), BF16))

    def mm(x, y):
        return _dot(x.astype(BF16), expand(y))

    refs = ((qf_ref, kf_ref, vf_ref, gf_ref), (qb_ref, kb_ref, vb_ref, gb_ref))
    chains = []
    for d in range(2):
        q_ref, k_ref, v_ref, g_ref = refs[d]
        raw = g_ref[...]
        comp = jnp.where(glane < 2 * GDN_HEADS, jax.nn.sigmoid(raw), -jnp.exp(alog) * _softplus(raw + dtb))
        gexp = _dot(comp, esel_ref[d], HIGHEST)
        if d == 0:
            incl, strict, upto, tri = jl <= row, jl < row, row <= jl, bd & (c2 <= r2)
        else:
            incl, strict, upto, tri = jl >= row, jl > row, row >= jl, bd & (c2 >= r2)
        last = CHUNK - 1 if d == 0 else 0
        g_all = gexp[:, w4:2 * w4]
        gcum_all = _dot(tri.astype(F32), g_all, HIGHEST)
        for c in range(NCH):
            rows = slice(c * CHUNK, (c + 1) * CHUNK)
            ch = dict(d=d, c=c, q=q_ref[rows, :], k=k_ref[rows, :], v=v_ref[rows, :], beta=gexp[rows, 0:w4],
                      gcum=gcum_all[rows], incl=incl, strict=strict, last=last)
            grow = jnp.sum(jnp.where(upto, g_all[rows], 0.0), axis=0, keepdims=True)
            ch["decay"] = jnp.where(incl, jnp.exp(jnp.minimum(ch["gcum"] - grow, 0.0)), 0.0)
            ch["kbeta"] = ch["k"] * ch["beta"]
            chains.append(ch)

    for ch in chains:
        prod = _dot_nt(jnp.concatenate([ch["kbeta"], ch["q"]], axis=0).astype(BF16), expand(ch["k"]))
        ch["n"] = jnp.where(ch["strict"], prod[0:CHUNK] * ch["decay"], 0.0)
        ch["qk"] = jnp.where(ch["incl"], prod[CHUNK:2 * CHUNK] * ch["decay"], 0.0)
        n16 = jnp.where(blk16, ch["n"], 0.0)
        ch["t"] = eye - n16
        ch["pw"] = n16
    for _ in range(3):
        for ch in chains:
            ch["pw"] = mm(ch["pw"], ch["pw"])
        for ch in chains:
            ch["t"] = ch["t"] + mm(ch["t"], ch["pw"])
    for off in (jnp.where(blk32 & ~blk16, 1.0, 0.0), jnp.where(blk32, 0.0, 1.0)):
        for ch in chains:
            ch["te"] = mm(ch["t"], ch["n"] * off)
        for ch in chains:
            ch["t"] = ch["t"] - mm(ch["te"], ch["t"])
    for ch in chains:
        eg = jnp.exp(ch["gcum"])
        ch["u"] = mm(ch["t"], ch["v"] * ch["beta"])
        w = mm(ch["t"], ch["kbeta"] * eg)
        glast = ch["gcum"][ch["last"]:ch["last"] + 1, :]
        ch["wq"] = jnp.concatenate([w, ch["q"] * eg], axis=0).astype(BF16)
        ch["kg"] = (ch["k"] * jnp.exp(glast - ch["gcum"])).astype(BF16)
        ch["eglast"] = jnp.exp(glast)

    state = [s_ref[0], s_ref[1]]
    res = {}
    for i in range(NCH):
        cur = [chains[i], chains[NCH + NCH - 1 - i]]
        ws = [_dot(ch["wq"], state[d].astype(BF16)) for d, ch in enumerate(cur)]
        v_new = [ch["u"] - ws[d][0:CHUNK] for d, ch in enumerate(cur)]
        for d, ch in enumerate(cur):
            res[(d, ch["c"])] = ws[d][CHUNK:2 * CHUNK] + mm(ch["qk"], v_new[d])
        for d, ch in enumerate(cur):
            upd = _dot_tn(ch["kg"], v_new[d].astype(BF16))
            state[d] = state[d] * ch["eglast"] + jnp.where(bd, upd, 0.0)
    outs = (of_ref, ob_ref)
    for d in range(2):
        s_ref[d] = state[d]
        for c in range(NCH):
            outs[d][c * CHUNK:(c + 1) * CHUNK, :] = res[(d, c)]


def _gdn_scan(qn, kn, vn, p, esel, gvec, layer):
    bsz, t, _ = p.shape
    nsteps = t // GB
    fwd = lambda b, i: (b, i, 0)
    bwd = lambda b, i: (b, jnp.where(i == 0, 0, nsteps - i), 0)
    gcol = C_GATE // LANE
    fwd_g = lambda b, i: (b, i, gcol)
    bwd_g = lambda b, i: (b, jnp.where(i == 0, 0, nsteps - i), gcol)
    blk = lambda im: pl.BlockSpec((None, GB, GDN_W), im)
    out = jax.ShapeDtypeStruct((bsz, t, GDN_W), F32)
    return pl.pallas_call(
        _gdn_scan_kernel,
        grid=(bsz, nsteps),
        in_specs=[blk(fwd), blk(fwd), blk(fwd), pl.BlockSpec((None, GB, LANE), fwd_g),
                  blk(bwd), blk(bwd), blk(bwd), pl.BlockSpec((None, GB, LANE), bwd_g),
                  pl.BlockSpec((2, LANE, 2 * GDN_W), lambda b, i: (0, 0, 0)),
                  pl.BlockSpec((None, 2, LANE), lambda b, i: (layer, 0, 0))],
        out_specs=[blk(fwd), blk(bwd)],
        out_shape=[out, out],
        scratch_shapes=[pltpu.VMEM((2, GDN_W, GDN_W), F32)],
        compiler_params=pltpu.CompilerParams(vmem_limit_bytes=VMEM_BIG),
        name="gdn_scan",
    )(qn, kn, vn, p, qn, kn, vn, p, esel, gvec)


def _outproj_kernel(x_ref, h_ref, nb_ref, of_ref, ob_ref, zg_ref, m_ref, w_ref, nw_ref, hsum_ref,
                    lng_ref, lnb_ref, o_ref):
    o = of_ref[...] + ob_ref[...]
    ms = _head_sum_sq(o, hsum_ref[...]) * (1.0 / HD)
    og = o * lax.rsqrt(ms + NORM_EPS) * nw_ref[...] * _silu(zg_ref[...])
    y = (_dot(h_ref[...].astype(BF16), w_ref[0:RG_W, :])
         + _dot(nb_ref[...].astype(BF16), w_ref[RG_W:RG_W + NA_W, :])
         + _dot(og.astype(BF16), w_ref[RG_W + NA_W:D, :]))
    gate = m_ref[...][:, 2 * D:3 * D]
    z = DEEPNORM_ALPHA * x_ref[...] + gate * y
    mu = jnp.mean(z, axis=-1, keepdims=True)
    zc = z - mu
    var = jnp.mean(zc * zc, axis=-1, keepdims=True)
    o_ref[...] = zc * lax.rsqrt(var + LN_EPS) * lng_ref[...] + lnb_ref[...]


def _out_proj(xa, hg, nb, o_f, o_b, p, modsel, w_out_b, nw4, hsum4, ln_g, ln_b, layer, latent_only):
    bsz, t, _ = xa.shape
    skip = LC // TM if latent_only else 0
    nt = t // TM - skip
    row = lambda width: pl.BlockSpec((None, TM, width), lambda b, i: (b, i + skip, 0))
    vec = lambda: pl.BlockSpec((None, 1, D), lambda b, i: (layer, 0, 0))
    return pl.pallas_call(
        _outproj_kernel,
        grid=(bsz, nt),
        in_specs=[row(D), row(RG_W), row(NA_W), row(GDN_W), row(GDN_W),
                  pl.BlockSpec((None, TM, GDN_W), lambda b, i: (b, i + skip, C_ZG // GDN_W)),
                  pl.BlockSpec((None, None, 1, 3 * D),
                               lambda b, i: (layer, 2 * b + jnp.minimum(i + skip, 1), 0, 0)),
                  pl.BlockSpec((None, D, D), lambda b, i: (layer, 0, 0)),
                  pl.BlockSpec((None, 1, GDN_W), lambda b, i: (layer, 0, 0)),
                  pl.BlockSpec((GDN_W, GDN_W), lambda b, i: (0, 0)),
                  vec(), vec()],
        out_specs=pl.BlockSpec((None, TM, D), lambda b, i: (b, i, 0)),
        out_shape=jax.ShapeDtypeStruct((bsz, nt * TM, D), F32),
        compiler_params=pltpu.CompilerParams(vmem_limit_bytes=VMEM_BIG),
        name="out_proj",
    )(xa, hg, nb, o_f, o_b, p, modsel, w_out_b, nw4, hsum4, ln_g, ln_b)


def _na_bias_table(na_rpb):
    jq = np.arange(GRID_W)
    col_start = np.clip(jq - NA_COLS // 2, 0, GRID_W - NA_COLS)
    kc = np.arange(GRID_W)
    inside = (kc[None, :] >= col_start[:, None]) & (kc[None, :] < col_start[:, None] + NA_COLS)
    col_off = np.clip(kc[None, :] - jq[:, None] + NA_COLS - 1, 0, 2 * NA_COLS - 2)
    tab = jnp.where(inside[None, None, None], na_rpb[:, :, :, col_off], NEG)
    tab = jnp.concatenate([tab, jnp.full((DEPTH, NA_HEADS, 1, GRID_W, GRID_W), NEG, F32)], axis=2)
    return jnp.concatenate([tab, tab], axis=-1)


def _rope_tables(seq):
    pos = jnp.arange(seq)
    rows_pos, cols_pos = pos // GRID_W, pos % GRID_W
    half = HD // 2
    nf = half // 2
    inv_freq = ROPE_BASE ** (-jnp.arange(nf, dtype=F32) / nf)
    lane = np.arange(LANE)
    jl = lane % HD
    use_row = jl < half
    f = (jl % half) % nf
    sign = np.where((jl % half) < nf, -1.0, 1.0).astype(np.float32)
    pos_l = jnp.where(use_row[None, :], rows_pos[:, None], cols_pos[:, None]).astype(F32)
    ang = pos_l * inv_freq[f][None, :]
    cos_t = jnp.concatenate([jnp.ones((LC, LANE), F32), jnp.cos(ang)], axis=0)
    sin_t = jnp.concatenate([jnp.zeros((LC, LANE), F32), jnp.sin(ang) * sign[None, :]], axis=0)
    return cos_t, sin_t


def _gate_select():
    e = np.zeros((2, LANE, 2 * GDN_W), np.float32)
    for d in range(2):
        for l in range(GDN_W):
            h = l // HD
            e[d, d * GDN_HEADS + h, l] = 1.0
            e[d, 2 * GDN_HEADS + d * GDN_HEADS + h, GDN_W + l] = 1.0
    return jnp.asarray(e)


def _head_sum(width):
    i = np.arange(width)
    return jnp.asarray((i[:, None] // HD == i[None, :] // HD).astype(np.float32)).astype(BF16)


def kernel(x, c, ctx, c_ctx, w_mod, b_mod, w_in, conv_w, rg_wa, rg_ba, rg_wx, rg_bx, rg_lam, na_rpb, gdn_alog,
           gdn_dtb, gdn_nw, w_out, ln_g, ln_b):
    bsz, seq, _ = x.shape
    assert ctx.shape[1] == LC and seq == GRID_W * GRID_W

    c8 = jnp.concatenate([c, c_ctx[None], jnp.zeros((SUB - bsz - 1, D), F32)], axis=0)
    w_in_p = jnp.pad(w_in, ((0, 0), (0, 0), (0, D_INP - D_IN))).astype(BF16)
    w_out_b = w_out.astype(BF16)
    ns = RG_W // LANE
    eye2 = jnp.eye(2, dtype=F32)

    def block_diag(wt):
        wt = wt.reshape(DEPTH, 2, ns, 2, RG_BLOCK, RG_BLOCK)
        return jnp.einsum('ldsjae,jk->ldsjake', wt, eye2).reshape(DEPTH, 2, ns, LANE, LANE)

    wa, wx = block_diag(rg_wa), block_diag(rg_wx)
    rg_w = jnp.concatenate([wa[:, 0], wx[:, 0], wa[:, 1], wx[:, 1]], axis=-1).astype(BF16)
    slab = lambda v: v.reshape(DEPTH, ns, 1, LANE)
    rg_b = jnp.concatenate([slab(rg_ba[:, 0]), slab(rg_bx[:, 0]), slab(rg_ba[:, 1]), slab(rg_bx[:, 1])], axis=-1)
    rg_lam2 = jnp.concatenate([slab(rg_lam[:, 0]), slab(rg_lam[:, 1])], axis=-1)
    conv_w3 = jnp.stack([conv_w[:, :, C_GQ:C_GK], conv_w[:, :, C_GK:C_GV], conv_w[:, :, C_GV:C_ZA]], axis=1)
    bias_tab = _na_bias_table(na_rpb)
    cos_t, sin_t = _rope_tables(seq)
    esel = _gate_select()
    pad = jnp.zeros((DEPTH, LANE - 4 * GDN_HEADS), F32)
    zero8 = jnp.zeros((DEPTH, 2 * GDN_HEADS), F32)
    gvec = jnp.stack([jnp.concatenate([zero8, gdn_alog.reshape(DEPTH, -1), pad], axis=-1),
                      jnp.concatenate([zero8, gdn_dtb.reshape(DEPTH, -1), pad], axis=-1)], axis=1)
    nw4 = jnp.tile(gdn_nw, (1, GDN_HEADS)).reshape(DEPTH, 1, GDN_W)
    hsum2, hsum4 = _head_sum(LANE), _head_sum(GDN_W)
    ln_g3, ln_b3 = ln_g.reshape(DEPTH, 1, D), ln_b.reshape(DEPTH, 1, D)

    mods = _modulation(c8, w_mod, b_mod)
    ctx_rows = jnp.broadcast_to(mods[:, bsz:bsz + 1], (DEPTH, bsz, 3 * D))
    modsel = jnp.stack([ctx_rows, mods[:, :bsz]], axis=2).reshape(DEPTH, 2 * bsz, 1, 3 * D)

    xa = jnp.concatenate([ctx, x], axis=1)
    for layer in range(DEPTH):
        p, kv16 = _in_proj(xa, modsel, w_in_p, layer)
        hg = _rglru(p, conv_w, rg_w, rg_b, rg_lam2, layer)
        nb = _natten(p, kv16, bias_tab, layer)
        qn, kn, vn = _gdn_prep(p, conv_w3, cos_t, sin_t, hsum2, layer)
        o_f, o_b = _gdn_scan(qn, kn, vn, p, esel, gvec, layer)
        xa = _out_proj(xa, hg, nb, o_f, o_b, p, modsel, w_out_b, nw4, hsum4, ln_g3, ln_b3, layer,
                       latent_only=layer == DEPTH - 1)
    return xa
```

```python
import functools

import numpy as np
import jax
import jax.numpy as jnp
from jax import lax
from jax.experimental import pallas as pl
from jax.experimental.pallas import tpu as pltpu

F32 = jnp.float32
BF16 = jnp.bfloat16
HIGHEST = lax.Precision.HIGHEST

D = 1024
DEPTH = 4
LC = 256
GRID_W = 64
CONV_K = 4
RG_W = 384
RG_BLOCK = 64
RG_C = 8.0
NA_HEADS = 6
NA_W = 384
NA_ROWS = 8
NA_COLS = 16
GDN_HEADS = 4
GDN_W = 256
HD = 64
CHUNK = 64
ROPE_BASE = 10000.0
DEEPNORM_ALPHA = (2.0 * DEPTH) ** 0.25
LN_EPS = 1e-5
NORM_EPS = 1e-6
NEG = -1e30

C_RGX = 0
C_GQ = 384
C_GK = 640
C_GV = 896
C_ZA = 1152
C_NQ = 1536
C_ZN = 1920
C_ZG = 2304
C_GATE = 2560
D_F32 = 2688
C_NK = 2688
C_NV = 3072
D_INP = 3456
N_GATE = 4 * GDN_HEADS


def _reorder_in_cols(w):
    nq_end = RG_W + 3 * GDN_W + RG_W + NA_W
    kv_end = nq_end + 2 * NA_W
    zg_end = kv_end + NA_W + GDN_W
    gates = w[..., zg_end:zg_end + N_GATE]
    pad = jnp.zeros(gates.shape[:-1] + (LANE - N_GATE,), w.dtype)
    return jnp.concatenate([w[..., :nq_end], w[..., kv_end:zg_end], gates, pad, w[..., nq_end:kv_end]], axis=-1)

LANE = 128
SUB = 8
TM = 256
RC = 128
VMEM_BIG = 56 * 1024 * 1024


def _sigmoid(x):
    return 0.5 * jnp.tanh(0.5 * x) + 0.5


def _silu(x):
    return x * _sigmoid(x)


def _softplus(x):
    return jnp.maximum(x, 0.0) + jnp.log1p(jnp.exp(-jnp.abs(x)))


def _dot(a, b, precision=None):
    return jnp.dot(a, b, preferred_element_type=F32, precision=precision)


def _split2(x):
    hi = x.astype(BF16)
    return hi, (x - hi.astype(F32)).astype(BF16)


def _dot_split(x, onehot):
    hi, lo = _split2(x)
    return _dot(hi, onehot) + _dot(lo, onehot)


def _dot_nt(a, b, precision=None):
    return lax.dot_general(a, b, (((1,), (1,)), ((), ())), preferred_element_type=F32, precision=precision)


def _dot_tn(a, b, precision=None):
    return lax.dot_general(a, b, (((0,), (0,)), ((), ())), preferred_element_type=F32, precision=precision)


def _mod_kernel(c_ref, w_ref, b_ref, o_ref):
    s = _silu(c_ref[...])
    o_ref[...] = _dot(s, w_ref[...], HIGHEST) + b_ref[...]


def _modulation(c8, w_mod, b_mod):
    nb = 3 * D // D
    return pl.pallas_call(
        _mod_kernel,
        grid=(DEPTH, nb),
        in_specs=[pl.BlockSpec((SUB, D), lambda l, j: (0, 0)),
                  pl.BlockSpec((None, D, D), lambda l, j: (l, 0, j)),
                  pl.BlockSpec((None, 1, D), lambda l, j: (l, 0, j))],
        out_specs=pl.BlockSpec((None, SUB, D), lambda l, j: (l, 0, j)),
        out_shape=jax.ShapeDtypeStruct((DEPTH, SUB, 3 * D), F32),
        name="modulation",
    )(c8, w_mod, b_mod.reshape(DEPTH, 1, 3 * D))


def _token_specs(tokens, skip=0):
    assert LC == TM
    if isinstance(tokens, tuple):
        return [pl.BlockSpec((None, LC, D), lambda b, i: (b, 0, 0)),
                pl.BlockSpec((None, TM, D), lambda b, i: (b, jnp.maximum(i + skip - 1, 0), 0))], list(tokens)
    return [pl.BlockSpec((None, TM, D), lambda b, i: (b, i + skip, 0))], [tokens]


def _load_tokens(refs, tile):
    if len(refs) == 2:
        return jnp.where(tile == 0, refs[0][...], refs[1][...])
    return refs[0][...]


def _token_shape(tokens):
    if isinstance(tokens, tuple):
        return tokens[1].shape[0], tokens[0].shape[1] + tokens[1].shape[1]
    return tokens.shape[0], tokens.shape[1]


def _inproj_kernel(*refs, n_tok):
    m_ref, w_ref, o_ref, kv_ref = refs[n_tok:]
    m = m_ref[...]
    shift = m[:, :D]
    scale = m[:, D:2 * D]
    u = _load_tokens(refs[:n_tok], pl.program_id(1)) * (1.0 + scale) + shift
    p = _dot(u.astype(BF16), w_ref[...])
    o_ref[...] = p[:, :D_F32]
    kv_ref[...] = p[:, C_NK:D_INP].astype(BF16)


def _in_proj(tokens, modsel, w_in_p, layer):
    bsz, t = _token_shape(tokens)
    nt = t // TM
    tok_specs, tok_ops = _token_specs(tokens)
    return pl.pallas_call(
        functools.partial(_inproj_kernel, n_tok=len(tok_ops)),
        grid=(bsz, nt),
        in_specs=tok_specs + [
                  pl.BlockSpec((None, None, 1, 3 * D), lambda b, i: (layer, 2 * b + jnp.minimum(i, 1), 0, 0)),
                  pl.BlockSpec((None, D, D_INP), lambda b, i: (layer, 0, 0))],
        out_specs=[pl.BlockSpec((None, TM, D_F32), lambda b, i: (b, i, 0)),
                   pl.BlockSpec((None, TM, 2 * NA_W), lambda b, i: (b, i, 0))],
        out_shape=[jax.ShapeDtypeStruct((bsz, t, D_F32), F32), jax.ShapeDtypeStruct((bsz, t, 2 * NA_W), BF16)],
        compiler_params=pltpu.CompilerParams(vmem_limit_bytes=VMEM_BIG),
        name="in_proj",
    )(*tok_ops, modsel, w_in_p)


PAD_ROWS = 3 * SUB


def _fill_padded(src_ref, pad_ref, t):
    zeros = jnp.zeros((SUB, LANE), F32)
    pad_ref[0:SUB, :] = zeros
    pad_ref[SUB + LC:2 * SUB + LC, :] = zeros
    pad_ref[t + 2 * SUB:t + 3 * SUB, :] = zeros

    def body(c, carry):
        src = pl.multiple_of(c * LC, LC)
        dst = pl.multiple_of(src + SUB + jnp.where(c >= 1, SUB, 0), SUB)
        pad_ref[pl.ds(dst, LC), :] = src_ref[pl.ds(src, LC), :]
        return carry

    lax.fori_loop(0, t // LC, body, 0)


def _conv_chunk(pad_ref, cw, t0):
    start = pl.multiple_of(t0 + jnp.where(t0 >= LC, SUB, 0), SUB)
    win = pad_ref[pl.ds(start, RC + 2 * SUB), :]
    n = RC + 2 * SUB
    acc = win[SUB:SUB + RC] * cw[2:3, :]
    for k, shift in ((0, 2), (1, 1), (3, n - 1)):
        acc = acc + pltpu.roll(win, shift, 0)[SUB:SUB + RC] * cw[k:k + 1, :]
    return acc


def _scan8(a, b, row, reverse):
    for s in (1, 2, 4):
        if reverse:
            sh, ok = SUB - s, row < SUB - s
        else:
            sh, ok = s, row >= s
        a_sh = jnp.where(ok, pltpu.roll(a, sh, 0), 1.0)
        b_sh = jnp.where(ok, pltpu.roll(b, sh, 0), 0.0)
        b = a * b_sh + b
        a = a * a_sh
    return a, b


def _rglru_kernel(x_ref, z_ref, cw_ref, w_ref, bias_ref, lam_ref, o_ref,
                  pad_ref, a0_ref, b0_ref, a1_ref, b1_ref, hf_ref, hb_ref):
    t = x_ref.shape[0]
    _fill_padded(x_ref, pad_ref, t)
    cw = cw_ref[...]
    w = w_ref[...]
    bias = bias_ref[...]
    ls = -_softplus(-lam_ref[...])

    def gates(c, carry):
        t0 = pl.multiple_of(c * RC, RC)
        u = _conv_chunk(pad_ref, cw, t0)
        g = _dot(u.astype(BF16), w) + bias
        for d, (a_ref, b_ref) in enumerate(((a0_ref, b0_ref), (a1_ref, b1_ref))):
            r = _sigmoid(g[:, (2 * d) * LANE:(2 * d + 1) * LANE])
            i = _sigmoid(g[:, (2 * d + 1) * LANE:(2 * d + 2) * LANE])
            log_a = RG_C * r * ls[:, d * LANE:(d + 1) * LANE]
            a = jnp.exp(log_a)
            mult = jnp.sqrt(-jnp.tanh(log_a) * (a * a + 1.0))
            a_ref[pl.ds(t0, RC), :] = a
            b_ref[pl.ds(t0, RC), :] = mult * (i * u)
        return carry

    lax.fori_loop(0, t // RC, gates, 0, unroll=2)

    row = lax.broadcasted_iota(jnp.int32, (SUB, LANE), 0)
    n_tiles = t // SUB
    n_ctx = LC // SUB

    def scan(i, carry):
        cf, cb = carry
        rf = pl.multiple_of(i * SUB, SUB)
        af, bf = _scan8(a0_ref[pl.ds(rf, SUB), :], b0_ref[pl.ds(rf, SUB), :], row, False)
        hf = bf + af * cf
        hf_ref[pl.ds(rf, SUB), :] = hf
        cf = jnp.broadcast_to(hf[SUB - 1:SUB, :], (SUB, LANE))
        j = jnp.where(i < n_ctx, n_ctx - 1 - i, n_tiles + n_ctx - 1 - i)
        rb = pl.multiple_of(j * SUB, SUB)
        ab, bb = _scan8(a1_ref[pl.ds(rb, SUB), :], b1_ref[pl.ds(rb, SUB), :], row, True)
        hb = bb + ab * cb
        hb_ref[pl.ds(rb, SUB), :] = hb
        cb = jnp.broadcast_to(hb[0:1, :], (SUB, LANE))
        return cf, cb

    zero = jnp.zeros((SUB, LANE), F32)
    lax.fori_loop(0, n_tiles, scan, (zero, zero), unroll=2)

    def finish(c, carry):
        t0 = pl.multiple_of(c * RC, RC)
        h = hf_ref[pl.ds(t0, RC), :] + hb_ref[pl.ds(t0, RC), :]
        o_ref[pl.ds(t0, RC), :] = h * _silu(z_ref[pl.ds(t0, RC), :])
        return carry

    lax.fori_loop(0, t // RC, finish, 0)


def _rglru(p, conv_w, rg_w, rg_b, rg_lam2, layer):
    bsz, t, _ = p.shape
    ns = RG_W // LANE
    seq = lambda: pltpu.VMEM((t, LANE), F32)
    return pl.pallas_call(
        _rglru_kernel,
        grid=(bsz, ns),
        in_specs=[pl.BlockSpec((None, t, LANE), lambda b, s: (b, 0, C_RGX // LANE + s)),
                  pl.BlockSpec((None, t, LANE), lambda b, s: (b, 0, C_ZA // LANE + s)),
                  pl.BlockSpec((None, CONV_K, LANE), lambda b, s: (layer, 0, C_RGX // LANE + s)),
                  pl.BlockSpec((None, None, LANE, 4 * LANE), lambda b, s: (layer, s, 0, 0)),
                  pl.BlockSpec((None, None, 1, 4 * LANE), lambda b, s: (layer, s, 0, 0)),
                  pl.BlockSpec((None, None, 1, 2 * LANE), lambda b, s: (layer, s, 0, 0))],
        out_specs=pl.BlockSpec((None, t, LANE), lambda b, s: (b, 0, s)),
        out_shape=jax.ShapeDtypeStruct((bsz, t, RG_W), F32),
        scratch_shapes=[pltpu.VMEM((t + PAD_ROWS, LANE), F32), seq(), seq(), seq(), seq(), seq(), seq()],
        compiler_params=pltpu.CompilerParams(vmem_limit_bytes=VMEM_BIG),
        name="rglru",
    )(p, p, conv_w, rg_w, rg_b, rg_lam2)


NA_QR = 2
NA_NQ = NA_QR * GRID_W
NA_KR = 10
NA_MASKED = 2 * NA_ROWS - 1


def _na_kernel(q_ref, kv_ref, z_ref, tab_ref, o_ref):
    j = pl.program_id(1)
    n_ctx_steps = LC // NA_NQ
    ns = NA_W // LANE
    lane = lax.broadcasted_iota(jnp.int32, (NA_NQ, LANE), 1)
    first = lane < HD
    first_blk = lax.broadcasted_iota(jnp.int32, (GRID_W, LANE), 1) < HD
    q = q_ref[...] * (HD ** -0.5)

    def stacked_q(s):
        qs = q[:, s * LANE:(s + 1) * LANE]
        return jnp.concatenate([jnp.where(first, qs, 0.0), jnp.where(first, 0.0, qs)], axis=0).astype(BF16)

    def finish(s, sc, vv, n_keys):
        m = jnp.max(sc, axis=-1, keepdims=True)
        pr = jnp.exp(sc - m)
        den = jnp.sum(pr, axis=-1, keepdims=True)
        pv = _dot(pr.astype(BF16), vv) * (1.0 / den)
        o = jnp.where(first, pv[0:NA_NQ], pv[NA_NQ:2 * NA_NQ])
        cols = slice(s * LANE, (s + 1) * LANE)
        o_ref[:, cols] = o * _silu(z_ref[:, cols])

    @pl.when(j >= n_ctx_steps)
    def _():
        r = NA_QR * (j - n_ctx_steps)
        r0 = [jnp.clip(r + i - NA_ROWS // 2, 0, GRID_W - NA_ROWS) for i in range(NA_QR)]
        start = jnp.minimum(r0[0], GRID_W - NA_KR)
        base = pl.multiple_of(LC + start * GRID_W, GRID_W)
        idx = [[jnp.where((start + wu >= r0[i]) & (start + wu < r0[i] + NA_ROWS),
                          start + wu - (r + i) + NA_ROWS - 1, NA_MASKED) for wu in range(NA_KR)]
               for i in range(NA_QR)]
        n_loc = NA_KR * GRID_W
        scores = []
        for s in range(ns):
            kk = jnp.concatenate([kv_ref[pl.ds(base, n_loc), s * LANE:(s + 1) * LANE],
                                  kv_ref[0:LC, s * LANE:(s + 1) * LANE]], axis=0)
            scores.append(_dot_nt(stacked_q(s), kk))
        for s in range(ns):
            rows = []
            for hh in range(2):
                h = 2 * s + hh
                for i in range(NA_QR):
                    rows.append(jnp.concatenate(
                        [jnp.where(first_blk, tab_ref[h, idx[i][2 * a]], tab_ref[h, idx[i][2 * a + 1]])
                         for a in range(NA_KR // 2)] + [jnp.zeros((GRID_W, LC), F32)], axis=1))
            scores[s] = scores[s] + jnp.concatenate(rows, axis=0)
        for s in range(ns):
            vv = jnp.concatenate([kv_ref[pl.ds(base, n_loc), NA_W + s * LANE:NA_W + (s + 1) * LANE],
                                  kv_ref[0:LC, NA_W + s * LANE:NA_W + (s + 1) * LANE]], axis=0)
            finish(s, scores[s], vv, n_loc + LC)

    @pl.when(j < n_ctx_steps)
    def _():
        scores = [_dot_nt(stacked_q(s), kv_ref[0:LC, s * LANE:(s + 1) * LANE]) for s in range(ns)]
        for s in range(ns):
            finish(s, scores[s], kv_ref[0:LC, NA_W + s * LANE:NA_W + (s + 1) * LANE], LC)


def _natten(p, kv16, bias_tab, layer):
    bsz, t, _ = p.shape
    nq = t // NA_NQ
    return pl.pallas_call(
        _na_kernel,
        grid=(bsz, nq),
        in_specs=[pl.BlockSpec((None, NA_NQ, NA_W), lambda b, j: (b, j, C_NQ // NA_W)),
                  pl.BlockSpec((None, t, 2 * NA_W), lambda b, j: (b, 0, 0)),
                  pl.BlockSpec((None, NA_NQ, NA_W), lambda b, j: (b, j, C_ZN // NA_W)),
                  pl.BlockSpec((None, NA_HEADS, 2 * NA_ROWS, GRID_W, LANE), lambda b, j: (layer, 0, 0, 0, 0))],
        out_specs=pl.BlockSpec((None, NA_NQ, NA_W), lambda b, j: (b, j, 0)),
        out_shape=jax.ShapeDtypeStruct((bsz, t, NA_W), F32),
        compiler_params=pltpu.CompilerParams(vmem_limit_bytes=VMEM_BIG),
        name="natten",
    )(p, kv16, p, bias_tab)


def _swap16(x):
    lane = lax.broadcasted_iota(jnp.int32, x.shape, 1)
    return jnp.where(lane % 32 < 16, pltpu.roll(x, LANE - 16, 1), pltpu.roll(x, 16, 1))


def _head_sum_sq(x, hsum):
    return _dot_split(x * x, hsum)


def _gdn_prep_kernel(q_ref, k_ref, v_ref, cw_ref, cos_ref, sin_ref, hsum_ref, qo_ref, ko_ref, vo_ref, pad_ref):
    t = q_ref.shape[0]
    hsum = hsum_ref[...]

    def one(src_ref, dst_ref, cw, mode):
        _fill_padded(src_ref, pad_ref, t)

        def body(c, carry):
            t0 = pl.multiple_of(c * RC, RC)
            x = _silu(_conv_chunk(pad_ref, cw, t0))
            if mode != "v":
                x = x * lax.rsqrt(_head_sum_sq(x, hsum) + NORM_EPS)
                x = x * cos_ref[pl.ds(t0, RC), :] + _swap16(x) * sin_ref[pl.ds(t0, RC), :]
            if mode == "q":
                x = x * (HD ** -0.5)
            dst_ref[pl.ds(t0, RC), :] = x
            return carry

        lax.fori_loop(0, t // RC, body, 0, unroll=2)

    one(q_ref, qo_ref, cw_ref[0], "q")
    one(k_ref, ko_ref, cw_ref[1], "k")
    one(v_ref, vo_ref, cw_ref[2], "v")


def _gdn_prep(p, conv_w3, cos_t, sin_t, hsum, layer):
    bsz, t, _ = p.shape
    ns = GDN_W // LANE
    col = lambda c0: (lambda b, s: (b, 0, c0 // LANE + s))
    out = jax.ShapeDtypeStruct((bsz, t, GDN_W), F32)
    return pl.pallas_call(
        _gdn_prep_kernel,
        grid=(bsz, ns),
        in_specs=[pl.BlockSpec((None, t, LANE), col(C_GQ)),
                  pl.BlockSpec((None, t, LANE), col(C_GK)),
                  pl.BlockSpec((None, t, LANE), col(C_GV)),
                  pl.BlockSpec((None, 3, CONV_K, LANE), lambda b, s: (layer, 0, 0, s)),
                  pl.BlockSpec((t, LANE), lambda b, s: (0, 0)),
                  pl.BlockSpec((t, LANE), lambda b, s: (0, 0)),
                  pl.BlockSpec((LANE, LANE), lambda b, s: (0, 0))],
        out_specs=[pl.BlockSpec((None, t, LANE), lambda b, s: (b, 0, s))] * 3,
        out_shape=[out, out, out],
        scratch_shapes=[pltpu.VMEM((t + PAD_ROWS, LANE), F32)],
        compiler_params=pltpu.CompilerParams(vmem_limit_bytes=VMEM_BIG),
        name="gdn_prep",
    )(p, p, p, conv_w3, cos_t, sin_t, hsum)


GB = 256
NCH = GB // CHUNK


def _gdn_scan_kernel(qf_ref, kf_ref, vf_ref, gf_ref, qb_ref, kb_ref, vb_ref, gb_ref,
                     esel_ref, gvec_ref, of_ref, ob_ref, s_ref):
    step = pl.program_id(1)
    w4 = GDN_W

    @pl.when(step == 0)
    def _():
        s_ref[...] = jnp.zeros_like(s_ref)

    row = lax.broadcasted_iota(jnp.int32, (CHUNK, w4), 0)
    jl = lax.broadcasted_iota(jnp.int32, (CHUNK, w4), 1) % HD
    r2 = lax.broadcasted_iota(jnp.int32, (w4, w4), 0)
    c2 = lax.broadcasted_iota(jnp.int32, (w4, w4), 1)
    bd = (r2 // HD) == (c2 // HD)
    eye = (jl == row).astype(F32)
    blk16 = (row // 16) == (jl // 16)
    blk32 = (row // 32) == (jl // 32)
    alog = gvec_ref[0:1, :]
    dtb = gvec_ref[1:2, :]
    glane = lax.broadcasted_iota(jnp.int32, (GB, LANE), 1)

    def expand(y):
        yb = y.astype(BF16)
        return jnp.where(bd, jnp.concatenate([yb, yb, yb, yb], axis=0), jnp.zeros((), BF16))

    def mm(x, y):
        return _dot(x.astype(BF16), expand(y))

    refs = ((qf_ref, kf_ref, vf_ref, gf_ref), (qb_ref, kb_ref, vb_ref, gb_ref))
    chains = []
    for d in range(2):
        q_ref, k_ref, v_ref, g_ref = refs[d]
        raw = g_ref[...]
        comp = jnp.where(glane < 2 * GDN_HEADS, _sigmoid(raw), -jnp.exp(alog) * _softplus(raw + dtb))
        gexp = _dot_split(comp, esel_ref[d])
        if d == 0:
            incl, strict, upto, tri = jl <= row, jl < row, row <= jl, bd & (c2 <= r2)
        else:
            incl, strict, upto, tri = jl >= row, jl > row, row >= jl, bd & (c2 >= r2)
        last = CHUNK - 1 if d == 0 else 0
        g_all = gexp[:, w4:2 * w4]
        g_hi, g_lo = _split2(g_all)
        tri = tri.astype(BF16)
        gcum_all = _dot(tri, g_hi) + _dot(tri, g_lo)
        for c in range(NCH):
            rows = slice(c * CHUNK, (c + 1) * CHUNK)
            ch = dict(d=d, c=c, q=q_ref[rows, :], k=k_ref[rows, :], v=v_ref[rows, :], beta=gexp[rows, 0:w4],
                      gcum=gcum_all[rows], incl=incl, strict=strict, last=last)
            grow = jnp.sum(jnp.where(upto, g_all[rows], 0.0), axis=0, keepdims=True)
            ch["decay"] = jnp.where(incl, jnp.exp(jnp.minimum(ch["gcum"] - grow, 0.0)), 0.0)
            ch["kbeta"] = ch["k"] * ch["beta"]
            chains.append(ch)

    for ch in chains:
        prod = _dot_nt(jnp.concatenate([ch["kbeta"], ch["q"]], axis=0).astype(BF16), expand(ch["k"]))
        ch["n"] = jnp.where(ch["strict"], prod[0:CHUNK] * ch["decay"], 0.0)
        ch["qk"] = jnp.where(ch["incl"], prod[CHUNK:2 * CHUNK] * ch["decay"], 0.0)
        n16 = jnp.where(blk16, ch["n"], 0.0)
        ch["t"] = eye - n16
        ch["pw"] = n16
    for ch in chains:
        ch["pw"] = mm(ch["pw"], ch["pw"])
    for _ in range(2):
        for ch in chains:
            both = mm(jnp.concatenate([ch["t"], ch["pw"]], axis=0), ch["pw"])
            ch["t"] = ch["t"] + both[0:CHUNK]
            ch["pw"] = both[CHUNK:2 * CHUNK]
    for ch in chains:
        ch["t"] = ch["t"] + mm(ch["t"], ch["pw"])
    for off in (jnp.where(blk32 & ~blk16, 1.0, 0.0), jnp.where(blk32, 0.0, 1.0)):
        for ch in chains:
            ch["te"] = mm(ch["t"], ch["n"] * off)
        for ch in chains:
            ch["t"] = ch["t"] - mm(ch["te"], ch["t"])
    for ch in chains:
        eg = jnp.exp(ch["gcum"])
        ch["u"] = mm(ch["t"], ch["v"] * ch["beta"])
        w = mm(ch["t"], ch["kbeta"] * eg)
        glast = ch["gcum"][ch["last"]:ch["last"] + 1, :]
        ch["wq"] = jnp.concatenate([w, ch["q"] * eg], axis=0).astype(BF16)
        ch["kg"] = (ch["k"] * jnp.exp(glast - ch["gcum"])).astype(BF16)
        ch["eglast"] = jnp.exp(glast)

    state = [s_ref[0], s_ref[1]]
    res = {}
    for i in range(NCH):
        cur = [chains[i], chains[NCH + NCH - 1 - i]]
        ws = [_dot(ch["wq"], state[d].astype(BF16)) for d, ch in enumerate(cur)]
        v_new = [ch["u"] - ws[d][0:CHUNK] for d, ch in enumerate(cur)]
        for d, ch in enumerate(cur):
            res[(d, ch["c"])] = ws[d][CHUNK:2 * CHUNK] + mm(ch["qk"], v_new[d])
        for d, ch in enumerate(cur):
            upd = _dot_tn(ch["kg"], v_new[d].astype(BF16))
            state[d] = state[d] * ch["eglast"] + jnp.where(bd, upd, 0.0)
    outs = (of_ref, ob_ref)
    for d in range(2):
        s_ref[d] = state[d]
        for c in range(NCH):
            outs[d][c * CHUNK:(c + 1) * CHUNK, :] = res[(d, c)]


def _gdn_scan(qn, kn, vn, p, esel, gvec, layer):
    bsz, t, _ = p.shape
    nsteps = t // GB
    fwd = lambda b, i: (b, i, 0)
    bwd = lambda b, i: (b, jnp.where(i == 0, 0, nsteps - i), 0)
    gcol = C_GATE // LANE
    fwd_g = lambda b, i: (b, i, gcol)
    bwd_g = lambda b, i: (b, jnp.where(i == 0, 0, nsteps - i), gcol)
    blk = lambda im: pl.BlockSpec((None, GB, GDN_W), im)
    out = jax.ShapeDtypeStruct((bsz, t, GDN_W), F32)
    return pl.pallas_call(
        _gdn_scan_kernel,
        grid=(bsz, nsteps),
        in_specs=[blk(fwd), blk(fwd), blk(fwd), pl.BlockSpec((None, GB, LANE), fwd_g),
                  blk(bwd), blk(bwd), blk(bwd), pl.BlockSpec((None, GB, LANE), bwd_g),
                  pl.BlockSpec((2, LANE, 2 * GDN_W), lambda b, i: (0, 0, 0)),
                  pl.BlockSpec((None, 2, LANE), lambda b, i: (layer, 0, 0))],
        out_specs=[blk(fwd), blk(bwd)],
        out_shape=[out, out],
        scratch_shapes=[pltpu.VMEM((2, GDN_W, GDN_W), F32)],
        compiler_params=pltpu.CompilerParams(vmem_limit_bytes=VMEM_BIG),
        name="gdn_scan",
    )(qn, kn, vn, p, qn, kn, vn, p, esel, gvec)


def _outproj_kernel(*refs, n_tok, skip):
    (h_ref, nb_ref, of_ref, ob_ref, zg_ref, m_ref, w_ref, nw_ref, hsum_ref, lng_ref, lnb_ref, o_ref) = refs[n_tok:]
    o = of_ref[...] + ob_ref[...]
    ms = _head_sum_sq(o, hsum_ref[...]) * (1.0 / HD)
    og = o * lax.rsqrt(ms + NORM_EPS) * nw_ref[...] * _silu(zg_ref[...])
    y = (_dot(h_ref[...].astype(BF16), w_ref[0:RG_W, :])
         + _dot(nb_ref[...].astype(BF16), w_ref[RG_W:RG_W + NA_W, :])
         + _dot(og.astype(BF16), w_ref[RG_W + NA_W:D, :]))
    gate = m_ref[...][:, 2 * D:3 * D]
    z = DEEPNORM_ALPHA * _load_tokens(refs[:n_tok], pl.program_id(1) + skip) + gate * y
    mu = jnp.mean(z, axis=-1, keepdims=True)
    zc = z - mu
    var = jnp.mean(zc * zc, axis=-1, keepdims=True)
    o_ref[...] = zc * lax.rsqrt(var + LN_EPS) * lng_ref[...] + lnb_ref[...]


def _out_proj(tokens, hg, nb, o_f, o_b, p, modsel, w_out_b, nw4, hsum4, ln_g, ln_b, layer, latent_only):
    bsz, t = _token_shape(tokens)
    skip = LC // TM if latent_only else 0
    nt = t // TM - skip
    tok_specs, tok_ops = _token_specs(tokens, skip)
    row = lambda width: pl.BlockSpec((None, TM, width), lambda b, i: (b, i + skip, 0))
    vec = lambda: pl.BlockSpec((None, 1, D), lambda b, i: (layer, 0, 0))
    return pl.pallas_call(
        functools.partial(_outproj_kernel, n_tok=len(tok_ops), skip=skip),
        grid=(bsz, nt),
        in_specs=tok_specs + [
                  row(RG_W), row(NA_W), row(GDN_W), row(GDN_W),
                  pl.BlockSpec((None, TM, GDN_W), lambda b, i: (b, i + skip, C_ZG // GDN_W)),
                  pl.BlockSpec((None, None, 1, 3 * D),
                               lambda b, i: (layer, 2 * b + jnp.minimum(i + skip, 1), 0, 0)),
                  pl.BlockSpec((None, D, D), lambda b, i: (layer, 0, 0)),
                  pl.BlockSpec((None, 1, GDN_W), lambda b, i: (layer, 0, 0)),
                  pl.BlockSpec((GDN_W, GDN_W), lambda b, i: (0, 0)),
                  vec(), vec()],
        out_specs=pl.BlockSpec((None, TM, D), lambda b, i: (b, i, 0)),
        out_shape=jax.ShapeDtypeStruct((bsz, nt * TM, D), F32),
        compiler_params=pltpu.CompilerParams(vmem_limit_bytes=VMEM_BIG),
        name="out_proj",
    )(*tok_ops, hg, nb, o_f, o_b, p, modsel, w_out_b, nw4, hsum4, ln_g, ln_b)


def _na_bias_table(na_rpb):
    jq = np.arange(GRID_W)
    col_start = np.clip(jq - NA_COLS // 2, 0, GRID_W - NA_COLS)
    kc = np.arange(GRID_W)
    inside = (kc[None, :] >= col_start[:, None]) & (kc[None, :] < col_start[:, None] + NA_COLS)
    left = GRID_W - NA_COLS
    padded = jnp.pad(na_rpb, ((0, 0), (0, 0), (0, 0), (left, 2 * GRID_W - left - (2 * NA_COLS - 1))))
    toeplitz = jnp.stack([padded[..., GRID_W - 1 - j:2 * GRID_W - 1 - j] for j in range(GRID_W)], axis=3)
    tab = jnp.where(inside[None, None, None], toeplitz, NEG)
    tab = jnp.concatenate([tab, jnp.full((DEPTH, NA_HEADS, 1, GRID_W, GRID_W), NEG, F32)], axis=2)
    return jnp.concatenate([tab, tab], axis=-1)


def _rope_tables(seq):
    pos = jnp.arange(seq)
    rows_pos, cols_pos = pos // GRID_W, pos % GRID_W
    half = HD // 2
    nf = half // 2
    inv_freq = ROPE_BASE ** (-jnp.arange(nf, dtype=F32) / nf)
    lane = np.arange(LANE)
    jl = lane % HD
    use_row = jl < half
    f = (jl % half) % nf
    sign = np.where((jl % half) < nf, -1.0, 1.0).astype(np.float32)
    pos_l = jnp.where(use_row[None, :], rows_pos[:, None], cols_pos[:, None]).astype(F32)
    ang = pos_l * inv_freq[f][None, :]
    cos_t = jnp.concatenate([jnp.ones((LC, LANE), F32), jnp.cos(ang)], axis=0)
    sin_t = jnp.concatenate([jnp.zeros((LC, LANE), F32), jnp.sin(ang) * sign[None, :]], axis=0)
    return cos_t, sin_t


def _gate_select():
    e = np.zeros((2, LANE, 2 * GDN_W), np.float32)
    for d in range(2):
        for l in range(GDN_W):
            h = l // HD
            e[d, d * GDN_HEADS + h, l] = 1.0
            e[d, 2 * GDN_HEADS + d * GDN_HEADS + h, GDN_W + l] = 1.0
    return jnp.asarray(e).astype(BF16)


def _head_sum(width):
    i = np.arange(width)
    return jnp.asarray((i[:, None] // HD == i[None, :] // HD).astype(np.float32)).astype(BF16)


def kernel(x, c, ctx, c_ctx, w_mod, b_mod, w_in, conv_w, rg_wa, rg_ba, rg_wx, rg_bx, rg_lam, na_rpb, gdn_alog,
           gdn_dtb, gdn_nw, w_out, ln_g, ln_b):
    bsz, seq, _ = x.shape
    assert ctx.shape[1] == LC and seq == GRID_W * GRID_W

    c8 = jnp.concatenate([c, c_ctx[None], jnp.zeros((SUB - bsz - 1, D), F32)], axis=0)
    w_in_p = _reorder_in_cols(w_in.astype(BF16))
    w_out_b = w_out.astype(BF16)
    ns = RG_W // LANE
    eye2 = jnp.eye(2, dtype=F32)

    def block_diag(wt):
        wt = wt.reshape(DEPTH, 2, ns, 2, RG_BLOCK, RG_BLOCK)
        return jnp.einsum('ldsjae,jk->ldsjake', wt, eye2).reshape(DEPTH, 2, ns, LANE, LANE)

    wa, wx = block_diag(rg_wa), block_diag(rg_wx)
    rg_w = jnp.concatenate([wa[:, 0], wx[:, 0], wa[:, 1], wx[:, 1]], axis=-1).astype(BF16)
    slab = lambda v: v.reshape(DEPTH, ns, 1, LANE)
    rg_b = jnp.concatenate([slab(rg_ba[:, 0]), slab(rg_bx[:, 0]), slab(rg_ba[:, 1]), slab(rg_bx[:, 1])], axis=-1)
    rg_lam2 = jnp.concatenate([slab(rg_lam[:, 0]), slab(rg_lam[:, 1])], axis=-1)
    conv_w3 = jnp.stack([conv_w[:, :, C_GQ:C_GK], conv_w[:, :, C_GK:C_GV], conv_w[:, :, C_GV:C_ZA]], axis=1)
    bias_tab = _na_bias_table(na_rpb)
    cos_t, sin_t = _rope_tables(seq)
    esel = _gate_select()
    pad = jnp.zeros((DEPTH, LANE - 4 * GDN_HEADS), F32)
    zero8 = jnp.zeros((DEPTH, 2 * GDN_HEADS), F32)
    gvec = jnp.stack([jnp.concatenate([zero8, gdn_alog.reshape(DEPTH, -1), pad], axis=-1),
                      jnp.concatenate([zero8, gdn_dtb.reshape(DEPTH, -1), pad], axis=-1)], axis=1)
    nw4 = jnp.tile(gdn_nw, (1, GDN_HEADS)).reshape(DEPTH, 1, GDN_W)
    hsum2, hsum4 = _head_sum(LANE), _head_sum(GDN_W)
    ln_g3, ln_b3 = ln_g.reshape(DEPTH, 1, D), ln_b.reshape(DEPTH, 1, D)

    mods = _modulation(c8, w_mod, b_mod)
    ctx_rows = jnp.broadcast_to(mods[:, bsz:bsz + 1], (DEPTH, bsz, 3 * D))
    modsel = jnp.stack([ctx_rows, mods[:, :bsz]], axis=2).reshape(DEPTH, 2 * bsz, 1, 3 * D)

    xa = (ctx, x)
    for layer in range(DEPTH):
        p, kv16 = _in_proj(xa, modsel, w_in_p, layer)
        hg = _rglru(p, conv_w, rg_w, rg_b, rg_lam2, layer)
        nb = _natten(p, kv16, bias_tab, layer)
        qn, kn, vn = _gdn_prep(p, conv_w3, cos_t, sin_t, hsum2, layer)
        o_f, o_b = _gdn_scan(qn, kn, vn, p, esel, gvec, layer)
        xa = _out_proj(xa, hg, nb, o_f, o_b, p, modsel, w_out_b, nw4, hsum4, ln_g3, ln_b3, layer,
                       latent_only=layer == DEPTH - 1)
    return xa
```

```python
import functools

import numpy as np
import jax
import jax.numpy as jnp
from jax import lax
from jax.experimental import pallas as pl
from jax.experimental.pallas import tpu as pltpu

F32 = jnp.float32
BF16 = jnp.bfloat16
HIGHEST = lax.Precision.HIGHEST

D = 1024
DEPTH = 4
LC = 256
GRID_W = 64
CONV_K = 4
RG_W = 384
RG_BLOCK = 64
RG_C = 8.0
NA_HEADS = 6
NA_W = 384
NA_ROWS = 8
NA_COLS = 16
GDN_HEADS = 4
GDN_W = 256
HD = 64
CHUNK = 64
ROPE_BASE = 10000.0
DEEPNORM_ALPHA = (2.0 * DEPTH) ** 0.25
LN_EPS = 1e-5
NORM_EPS = 1e-6
NEG = -1e30

C_RGX = 0
C_GQ = 384
C_GK = 640
C_GV = 896
C_ZA = 1152
C_NQ = 1536
C_ZN = 1920
C_ZG = 2304
C_GATE = 2560
D_F32 = 2688
C_NK = 2688
C_NV = 3072
D_INP = 3456
N_GATE = 4 * GDN_HEADS


def _reorder_in_cols(w):
    nq_end = RG_W + 3 * GDN_W + RG_W + NA_W
    kv_end = nq_end + 2 * NA_W
    zg_end = kv_end + NA_W + GDN_W
    gates = w[..., zg_end:zg_end + N_GATE]
    pad = jnp.zeros(gates.shape[:-1] + (LANE - N_GATE,), w.dtype)
    return jnp.concatenate([w[..., :nq_end], w[..., kv_end:zg_end], gates, pad, w[..., nq_end:kv_end]], axis=-1)

LANE = 128
SUB = 8
TM = 256
RC = 128
VMEM_BIG = 56 * 1024 * 1024


def _sigmoid(x):
    return 0.5 * jnp.tanh(0.5 * x) + 0.5


def _silu(x):
    return x * _sigmoid(x)


def _softplus(x):
    return jnp.maximum(x, 0.0) + jnp.log1p(jnp.exp(-jnp.abs(x)))


def _dot(a, b, precision=None):
    return jnp.dot(a, b, preferred_element_type=F32, precision=precision)


def _split2(x):
    hi = x.astype(BF16)
    return hi, (x - hi.astype(F32)).astype(BF16)


def _dot_split(x, onehot):
    hi, lo = _split2(x)
    return _dot(hi, onehot) + _dot(lo, onehot)


def _dot_nt(a, b, precision=None):
    return lax.dot_general(a, b, (((1,), (1,)), ((), ())), preferred_element_type=F32, precision=precision)


def _dot_tn(a, b, precision=None):
    return lax.dot_general(a, b, (((0,), (0,)), ((), ())), preferred_element_type=F32, precision=precision)


def _mod_kernel(c_ref, w_ref, b_ref, o_ref):
    s = _silu(c_ref[...])
    o_ref[...] = _dot(s, w_ref[...], HIGHEST) + b_ref[...]


def _modulation(c8, w_mod, b_mod):
    nb = 3 * D // D
    return pl.pallas_call(
        _mod_kernel,
        grid=(DEPTH, nb),
        in_specs=[pl.BlockSpec((SUB, D), lambda l, j: (0, 0)),
                  pl.BlockSpec((None, D, D), lambda l, j: (l, 0, j)),
                  pl.BlockSpec((None, 1, D), lambda l, j: (l, 0, j))],
        out_specs=pl.BlockSpec((None, SUB, D), lambda l, j: (l, 0, j)),
        out_shape=jax.ShapeDtypeStruct((DEPTH, SUB, 3 * D), F32),
        name="modulation",
    )(c8, w_mod, b_mod.reshape(DEPTH, 1, 3 * D))


def _token_specs(tokens, skip=0):
    assert LC == TM
    if isinstance(tokens, tuple):
        return [pl.BlockSpec((None, LC, D), lambda b, i: (b, 0, 0)),
                pl.BlockSpec((None, TM, D), lambda b, i: (b, jnp.maximum(i + skip - 1, 0), 0))], list(tokens)
    return [pl.BlockSpec((None, TM, D), lambda b, i: (b, i + skip, 0))], [tokens]


def _load_tokens(refs, tile):
    if len(refs) == 2:
        return jnp.where(tile == 0, refs[0][...], refs[1][...])
    return refs[0][...]


def _token_shape(tokens):
    if isinstance(tokens, tuple):
        return tokens[1].shape[0], tokens[0].shape[1] + tokens[1].shape[1]
    return tokens.shape[0], tokens.shape[1]


def _inproj_kernel(*refs, n_tok):
    m_ref, w_ref, o_ref, kv_ref = refs[n_tok:]
    m = m_ref[...]
    shift = m[:, :D]
    scale = m[:, D:2 * D]
    u = _load_tokens(refs[:n_tok], pl.program_id(1)) * (1.0 + scale) + shift
    p = _dot(u.astype(BF16), w_ref[...])
    o_ref[...] = p[:, :D_F32]
    kv_ref[...] = p[:, C_NK:D_INP].astype(BF16)


def _in_proj(tokens, modsel, w_in_p, layer):
    bsz, t = _token_shape(tokens)
    nt = t // TM
    tok_specs, tok_ops = _token_specs(tokens)
    return pl.pallas_call(
        functools.partial(_inproj_kernel, n_tok=len(tok_ops)),
        grid=(bsz, nt),
        in_specs=tok_specs + [
                  pl.BlockSpec((None, None, 1, 3 * D), lambda b, i: (layer, 2 * b + jnp.minimum(i, 1), 0, 0)),
                  pl.BlockSpec((None, D, D_INP), lambda b, i: (layer, 0, 0))],
        out_specs=[pl.BlockSpec((None, TM, D_F32), lambda b, i: (b, i, 0)),
                   pl.BlockSpec((None, TM, 2 * NA_W), lambda b, i: (b, i, 0))],
        out_shape=[jax.ShapeDtypeStruct((bsz, t, D_F32), F32), jax.ShapeDtypeStruct((bsz, t, 2 * NA_W), BF16)],
        compiler_params=pltpu.CompilerParams(vmem_limit_bytes=VMEM_BIG),
        name="in_proj",
    )(*tok_ops, modsel, w_in_p)


PAD_ROWS = 3 * SUB


def _fill_padded(src_ref, pad_ref, t):
    zeros = jnp.zeros((SUB, LANE), F32)
    pad_ref[0:SUB, :] = zeros
    pad_ref[SUB + LC:2 * SUB + LC, :] = zeros
    pad_ref[t + 2 * SUB:t + 3 * SUB, :] = zeros

    def body(c, carry):
        src = pl.multiple_of(c * LC, LC)
        dst = pl.multiple_of(src + SUB + jnp.where(c >= 1, SUB, 0), SUB)
        pad_ref[pl.ds(dst, LC), :] = src_ref[pl.ds(src, LC), :]
        return carry

    lax.fori_loop(0, t // LC, body, 0)


def _conv_chunk(pad_ref, cw, t0):
    base = pl.multiple_of(t0 + jnp.where(t0 >= LC, SUB, 0), SUB) + SUB
    acc = pad_ref[pl.ds(base, RC), :] * cw[2:3, :]
    for k in (0, 1, 3):
        acc = acc + pad_ref[pl.ds(base + (k - CONV_K // 2), RC), :] * cw[k:k + 1, :]
    return acc


def _scan8(a, b, row, reverse):
    for s in (1, 2, 4):
        if reverse:
            sh, ok = SUB - s, row < SUB - s
        else:
            sh, ok = s, row >= s
        a_sh = jnp.where(ok, pltpu.roll(a, sh, 0), 1.0)
        b_sh = jnp.where(ok, pltpu.roll(b, sh, 0), 0.0)
        b = a * b_sh + b
        a = a * a_sh
    return a, b


def _rglru_kernel(x_ref, z_ref, cw_ref, w_ref, bias_ref, lam_ref, o_ref,
                  pad_ref, a0_ref, b0_ref, a1_ref, b1_ref, hf_ref, hb_ref):
    t = x_ref.shape[0]
    _fill_padded(x_ref, pad_ref, t)
    cw = cw_ref[...]
    w = w_ref[...]
    bias = bias_ref[...]
    ls = -_softplus(-lam_ref[...])

    def gates(c, carry):
        t0 = pl.multiple_of(c * RC, RC)
        u = _conv_chunk(pad_ref, cw, t0)
        g = _dot(u.astype(BF16), w) + bias
        for d, (a_ref, b_ref) in enumerate(((a0_ref, b0_ref), (a1_ref, b1_ref))):
            r = _sigmoid(g[:, (2 * d) * LANE:(2 * d + 1) * LANE])
            i = _sigmoid(g[:, (2 * d + 1) * LANE:(2 * d + 2) * LANE])
            log_a = RG_C * r * ls[:, d * LANE:(d + 1) * LANE]
            a = jnp.exp(log_a)
            mult = jnp.sqrt(-jnp.tanh(log_a) * (a * a + 1.0))
            a_ref[pl.ds(t0, RC), :] = a
            b_ref[pl.ds(t0, RC), :] = mult * (i * u)
        return carry

    lax.fori_loop(0, t // RC, gates, 0, unroll=2)

    row = lax.broadcasted_iota(jnp.int32, (SUB, LANE), 0)
    n_tiles = t // SUB
    n_ctx = LC // SUB

    def scan(i, carry):
        cf, cb = carry
        rf = pl.multiple_of(i * SUB, SUB)
        af, bf = _scan8(a0_ref[pl.ds(rf, SUB), :], b0_ref[pl.ds(rf, SUB), :], row, False)
        hf = bf + af * cf
        hf_ref[pl.ds(rf, SUB), :] = hf
        cf = jnp.broadcast_to(hf[SUB - 1:SUB, :], (SUB, LANE))
        j = jnp.where(i < n_ctx, n_ctx - 1 - i, n_tiles + n_ctx - 1 - i)
        rb = pl.multiple_of(j * SUB, SUB)
        ab, bb = _scan8(a1_ref[pl.ds(rb, SUB), :], b1_ref[pl.ds(rb, SUB), :], row, True)
        hb = bb + ab * cb
        hb_ref[pl.ds(rb, SUB), :] = hb
        cb = jnp.broadcast_to(hb[0:1, :], (SUB, LANE))
        return cf, cb

    zero = jnp.zeros((SUB, LANE), F32)
    lax.fori_loop(0, n_tiles, scan, (zero, zero), unroll=2)

    def finish(c, carry):
        t0 = pl.multiple_of(c * RC, RC)
        h = hf_ref[pl.ds(t0, RC), :] + hb_ref[pl.ds(t0, RC), :]
        o_ref[pl.ds(t0, RC), :] = h * _silu(z_ref[pl.ds(t0, RC), :])
        return carry

    lax.fori_loop(0, t // RC, finish, 0)


def _rglru(p, conv_w, rg_w, rg_b, rg_lam2, layer):
    bsz, t, _ = p.shape
    ns = RG_W // LANE
    seq = lambda: pltpu.VMEM((t, LANE), F32)
    return pl.pallas_call(
        _rglru_kernel,
        grid=(bsz, ns),
        in_specs=[pl.BlockSpec((None, t, LANE), lambda b, s: (b, 0, C_RGX // LANE + s)),
                  pl.BlockSpec((None, t, LANE), lambda b, s: (b, 0, C_ZA // LANE + s)),
                  pl.BlockSpec((None, CONV_K, LANE), lambda b, s: (layer, 0, C_RGX // LANE + s)),
                  pl.BlockSpec((None, None, LANE, 4 * LANE), lambda b, s: (layer, s, 0, 0)),
                  pl.BlockSpec((None, None, 1, 4 * LANE), lambda b, s: (layer, s, 0, 0)),
                  pl.BlockSpec((None, None, 1, 2 * LANE), lambda b, s: (layer, s, 0, 0))],
        out_specs=pl.BlockSpec((None, t, LANE), lambda b, s: (b, 0, s)),
        out_shape=jax.ShapeDtypeStruct((bsz, t, RG_W), F32),
        scratch_shapes=[pltpu.VMEM((t + PAD_ROWS, LANE), F32), seq(), seq(), seq(), seq(), seq(), seq()],
        compiler_params=pltpu.CompilerParams(vmem_limit_bytes=VMEM_BIG),
        name="rglru",
    )(p, p, conv_w, rg_w, rg_b, rg_lam2)


NA_QR = 2
NA_NQ = NA_QR * GRID_W
NA_GROUPS = 2
NA_BQ = NA_GROUPS * NA_NQ
NA_KR = 10
NA_MASKED = 2 * NA_ROWS - 1


def _na_kernel(q_ref, kv_ref, z_ref, tab_ref, o_ref):
    j = pl.program_id(1)
    n_ctx_steps = LC // NA_BQ
    ns = NA_W // LANE
    lane = lax.broadcasted_iota(jnp.int32, (NA_NQ, LANE), 1)
    first = lane < HD
    first_blk = lax.broadcasted_iota(jnp.int32, (GRID_W, LANE), 1) < HD
    chains = [(g, s) for g in range(NA_GROUPS) for s in range(ns)]

    def stacked_q(g, s):
        qs = q_ref[g * NA_NQ:(g + 1) * NA_NQ, s * LANE:(s + 1) * LANE] * (HD ** -0.5)
        return jnp.concatenate([jnp.where(first, qs, 0.0), jnp.where(first, 0.0, qs)], axis=0).astype(BF16)

    def finish(g, s, sc, vv):
        m = jnp.max(sc, axis=-1, keepdims=True)
        pr = jnp.exp(sc - m)
        den = jnp.sum(pr, axis=-1, keepdims=True)
        pv = _dot(pr.astype(BF16), vv) * (1.0 / den)
        o = jnp.where(first, pv[0:NA_NQ], pv[NA_NQ:2 * NA_NQ])
        rows, cols = slice(g * NA_NQ, (g + 1) * NA_NQ), slice(s * LANE, (s + 1) * LANE)
        o_ref[rows, cols] = o * _silu(z_ref[rows, cols])

    @pl.when(j >= n_ctx_steps)
    def _():
        n_loc = NA_KR * GRID_W
        base, idx = [], []
        for g in range(NA_GROUPS):
            r = NA_QR * (NA_GROUPS * (j - n_ctx_steps) + g)
            r0 = [jnp.clip(r + i - NA_ROWS // 2, 0, GRID_W - NA_ROWS) for i in range(NA_QR)]
            start = jnp.minimum(r0[0], GRID_W - NA_KR)
            base.append(pl.multiple_of(LC + start * GRID_W, GRID_W))
            idx.append([[jnp.where((start + wu >= r0[i]) & (start + wu < r0[i] + NA_ROWS),
                                   start + wu - (r + i) + NA_ROWS - 1, NA_MASKED) for wu in range(NA_KR)]
                        for i in range(NA_QR)])
        scores = {}
        for g, s in chains:
            kk = jnp.concatenate([kv_ref[pl.ds(base[g], n_loc), s * LANE:(s + 1) * LANE],
                                  kv_ref[0:LC, s * LANE:(s + 1) * LANE]], axis=0)
            scores[g, s] = _dot_nt(stacked_q(g, s), kk)
        for g, s in chains:
            rows = []
            for hh in range(2):
                h = 2 * s + hh
                for i in range(NA_QR):
                    rows.append(jnp.concatenate(
                        [jnp.where(first_blk, tab_ref[h, idx[g][i][2 * a]], tab_ref[h, idx[g][i][2 * a + 1]])
                         for a in range(NA_KR // 2)] + [jnp.zeros((GRID_W, LC), F32)], axis=1))
            scores[g, s] = scores[g, s] + jnp.concatenate(rows, axis=0)
        for g, s in chains:
            vv = jnp.concatenate([kv_ref[pl.ds(base[g], n_loc), NA_W + s * LANE:NA_W + (s + 1) * LANE],
                                  kv_ref[0:LC, NA_W + s * LANE:NA_W + (s + 1) * LANE]], axis=0)
            finish(g, s, scores[g, s], vv)

    @pl.when(j < n_ctx_steps)
    def _():
        scores = {(g, s): _dot_nt(stacked_q(g, s), kv_ref[0:LC, s * LANE:(s + 1) * LANE]) for g, s in chains}
        for g, s in chains:
            finish(g, s, scores[g, s], kv_ref[0:LC, NA_W + s * LANE:NA_W + (s + 1) * LANE])


def _natten(p, kv16, bias_tab, layer):
    bsz, t, _ = p.shape
    nq = t // NA_BQ
    return pl.pallas_call(
        _na_kernel,
        grid=(bsz, nq),
        in_specs=[pl.BlockSpec((None, NA_BQ, NA_W), lambda b, j: (b, j, C_NQ // NA_W)),
                  pl.BlockSpec((None, t, 2 * NA_W), lambda b, j: (b, 0, 0)),
                  pl.BlockSpec((None, NA_BQ, NA_W), lambda b, j: (b, j, C_ZN // NA_W)),
                  pl.BlockSpec((None, NA_HEADS, 2 * NA_ROWS, GRID_W, LANE), lambda b, j: (layer, 0, 0, 0, 0))],
        out_specs=pl.BlockSpec((None, NA_BQ, NA_W), lambda b, j: (b, j, 0)),
        out_shape=jax.ShapeDtypeStruct((bsz, t, NA_W), F32),
        compiler_params=pltpu.CompilerParams(vmem_limit_bytes=VMEM_BIG),
        name="natten",
    )(p, kv16, p, bias_tab)


def _swap16(x):
    lane = lax.broadcasted_iota(jnp.int32, x.shape, 1)
    return jnp.where(lane % 32 < 16, pltpu.roll(x, LANE - 16, 1), pltpu.roll(x, 16, 1))


def _head_sum_sq(x, hsum):
    return _dot_split(x * x, hsum)


def _gdn_prep_kernel(q_ref, k_ref, v_ref, cw_ref, cos_ref, sin_ref, hsum_ref, qo_ref, ko_ref, vo_ref,
                     qpad_ref, kpad_ref, vpad_ref):
    t = q_ref.shape[0]
    hsum = hsum_ref[...]
    streams = ((q_ref, qpad_ref, qo_ref, "q"), (k_ref, kpad_ref, ko_ref, "k"), (v_ref, vpad_ref, vo_ref, "v"))
    for src_ref, pad_ref, _, _ in streams:
        _fill_padded(src_ref, pad_ref, t)

    def body(c, carry):
        t0 = pl.multiple_of(c * RC, RC)
        xs = [_silu(_conv_chunk(pad_ref, cw_ref[n], t0)) for n, (_, pad_ref, _, _) in enumerate(streams)]
        ssq = [_head_sum_sq(x, hsum) for x in xs[:2]]
        for n, (_, _, dst_ref, mode) in enumerate(streams):
            x = xs[n]
            if mode != "v":
                x = x * lax.rsqrt(ssq[n] + NORM_EPS)
                x = x * cos_ref[pl.ds(t0, RC), :] + _swap16(x) * sin_ref[pl.ds(t0, RC), :]
            if mode == "q":
                x = x * (HD ** -0.5)
            dst_ref[pl.ds(t0, RC), :] = x
        return carry

    lax.fori_loop(0, t // RC, body, 0, unroll=2)


def _gdn_prep(p, conv_w3, cos_t, sin_t, hsum, layer):
    bsz, t, _ = p.shape
    ns = GDN_W // LANE
    col = lambda c0: (lambda b, s: (b, 0, c0 // LANE + s))
    out = jax.ShapeDtypeStruct((bsz, t, GDN_W), F32)
    return pl.pallas_call(
        _gdn_prep_kernel,
        grid=(bsz, ns),
        in_specs=[pl.BlockSpec((None, t, LANE), col(C_GQ)),
                  pl.BlockSpec((None, t, LANE), col(C_GK)),
                  pl.BlockSpec((None, t, LANE), col(C_GV)),
                  pl.BlockSpec((None, 3, CONV_K, LANE), lambda b, s: (layer, 0, 0, s)),
                  pl.BlockSpec((t, LANE), lambda b, s: (0, 0)),
                  pl.BlockSpec((t, LANE), lambda b, s: (0, 0)),
                  pl.BlockSpec((LANE, LANE), lambda b, s: (0, 0))],
        out_specs=[pl.BlockSpec((None, t, LANE), lambda b, s: (b, 0, s))] * 3,
        out_shape=[out, out, out],
        scratch_shapes=[pltpu.VMEM((t + PAD_ROWS, LANE), F32)] * 3,
        compiler_params=pltpu.CompilerParams(vmem_limit_bytes=VMEM_BIG),
        name="gdn_prep",
    )(p, p, p, conv_w3, cos_t, sin_t, hsum)


GB = 256
NCH = GB // CHUNK


def _gdn_scan_kernel(qf_ref, kf_ref, vf_ref, gf_ref, qb_ref, kb_ref, vb_ref, gb_ref,
                     esel_ref, gvec_ref, of_ref, ob_ref, s_ref):
    step = pl.program_id(1)
    w4 = GDN_W

    @pl.when(step == 0)
    def _():
        s_ref[...] = jnp.zeros_like(s_ref)

    row = lax.broadcasted_iota(jnp.int32, (CHUNK, w4), 0)
    jl = lax.broadcasted_iota(jnp.int32, (CHUNK, w4), 1) % HD
    r2 = lax.broadcasted_iota(jnp.int32, (w4, w4), 0)
    c2 = lax.broadcasted_iota(jnp.int32, (w4, w4), 1)
    bd = (r2 // HD) == (c2 // HD)
    eye = (jl == row).astype(F32)
    blk16 = (row // 16) == (jl // 16)
    blk32 = (row // 32) == (jl // 32)
    alog = gvec_ref[0:1, :]
    dtb = gvec_ref[1:2, :]
    glane = lax.broadcasted_iota(jnp.int32, (GB, LANE), 1)

    def expand(y):
        yb = y.astype(BF16)
        return jnp.where(bd, jnp.concatenate([yb, yb, yb, yb], axis=0), jnp.zeros((), BF16))

    def mm(x, y):
        return _dot(x.astype(BF16), expand(y))

    refs = ((qf_ref, kf_ref, vf_ref, gf_ref), (qb_ref, kb_ref, vb_ref, gb_ref))
    chains = []
    for d in range(2):
        q_ref, k_ref, v_ref, g_ref = refs[d]
        raw = g_ref[...]
        comp = jnp.where(glane < 2 * GDN_HEADS, _sigmoid(raw), -jnp.exp(alog) * _softplus(raw + dtb))
        gexp = _dot_split(comp, esel_ref[d])
        if d == 0:
            incl, strict, upto, tri = jl <= row, jl < row, row <= jl, bd & (c2 <= r2)
        else:
            incl, strict, upto, tri = jl >= row, jl > row, row >= jl, bd & (c2 >= r2)
        last = CHUNK - 1 if d == 0 else 0
        g_all = gexp[:, w4:2 * w4]
        g_hi, g_lo = _split2(g_all)
        tri = tri.astype(BF16)
        gcum_all = _dot(tri, g_hi) + _dot(tri, g_lo)
        for c in range(NCH):
            rows = slice(c * CHUNK, (c + 1) * CHUNK)
            ch = dict(d=d, c=c, q=q_ref[rows, :], k=k_ref[rows, :], v=v_ref[rows, :], beta=gexp[rows, 0:w4],
                      gcum=gcum_all[rows], incl=incl, strict=strict, last=last)
            grow = jnp.sum(jnp.where(upto, g_all[rows], 0.0), axis=0, keepdims=True)
            ch["decay"] = jnp.where(incl, jnp.exp(jnp.minimum(ch["gcum"] - grow, 0.0)), 0.0)
            ch["kbeta"] = ch["k"] * ch["beta"]
            chains.append(ch)

    for ch in chains:
        prod = _dot_nt(jnp.concatenate([ch["kbeta"], ch["q"]], axis=0).astype(BF16), expand(ch["k"]))
        ch["n"] = jnp.where(ch["strict"], prod[0:CHUNK] * ch["decay"], 0.0)
        ch["qk"] = jnp.where(ch["incl"], prod[CHUNK:2 * CHUNK] * ch["decay"], 0.0)
        n16 = jnp.where(blk16, ch["n"], 0.0)
        ch["t"] = eye - n16
        ch["pw"] = n16
    for ch in chains:
        ch["pw"] = mm(ch["pw"], ch["pw"])
    for _ in range(2):
        for ch in chains:
            both = mm(jnp.concatenate([ch["t"], ch["pw"]], axis=0), ch["pw"])
            ch["t"] = ch["t"] + both[0:CHUNK]
            ch["pw"] = both[CHUNK:2 * CHUNK]
    for ch in chains:
        ch["t"] = ch["t"] + mm(ch["t"], ch["pw"])
    for off in (jnp.where(blk32 & ~blk16, 1.0, 0.0), jnp.where(blk32, 0.0, 1.0)):
        for ch in chains:
            ch["te"] = mm(ch["t"], ch["n"] * off)
        for ch in chains:
            ch["t"] = ch["t"] - mm(ch["te"], ch["t"])
    for ch in chains:
        eg = jnp.exp(ch["gcum"])
        ch["u"] = mm(ch["t"], ch["v"] * ch["beta"])
        w = mm(ch["t"], ch["kbeta"] * eg)
        glast = ch["gcum"][ch["last"]:ch["last"] + 1, :]
        ch["wq"] = jnp.concatenate([w, ch["q"] * eg], axis=0).astype(BF16)
        ch["kg"] = (ch["k"] * jnp.exp(glast - ch["gcum"])).astype(BF16)
        ch["eglast"] = jnp.exp(glast)

    state = [s_ref[0], s_ref[1]]
    res = {}
    for i in range(NCH):
        cur = [chains[i], chains[NCH + NCH - 1 - i]]
        ws = [_dot(ch["wq"], state[d].astype(BF16)) for d, ch in enumerate(cur)]
        v_new = [ch["u"] - ws[d][0:CHUNK] for d, ch in enumerate(cur)]
        for d, ch in enumerate(cur):
            res[(d, ch["c"])] = ws[d][CHUNK:2 * CHUNK] + mm(ch["qk"], v_new[d])
        for d, ch in enumerate(cur):
            upd = _dot_tn(ch["kg"], v_new[d].astype(BF16))
            state[d] = state[d] * ch["eglast"] + jnp.where(bd, upd, 0.0)
    outs = (of_ref, ob_ref)
    for d in range(2):
        s_ref[d] = state[d]
        for c in range(NCH):
            outs[d][c * CHUNK:(c + 1) * CHUNK, :] = res[(d, c)]


def _gdn_scan(qn, kn, vn, p, esel, gvec, layer):
    bsz, t, _ = p.shape
    nsteps = t // GB
    fwd = lambda b, i: (b, i, 0)
    bwd = lambda b, i: (b, jnp.where(i == 0, 0, nsteps - i), 0)
    gcol = C_GATE // LANE
    fwd_g = lambda b, i: (b, i, gcol)
    bwd_g = lambda b, i: (b, jnp.where(i == 0, 0, nsteps - i), gcol)
    blk = lambda im: pl.BlockSpec((None, GB, GDN_W), im)
    out = jax.ShapeDtypeStruct((bsz, t, GDN_W), F32)
    return pl.pallas_call(
        _gdn_scan_kernel,
        grid=(bsz, nsteps),
        in_specs=[blk(fwd), blk(fwd), blk(fwd), pl.BlockSpec((None, GB, LANE), fwd_g),
                  blk(bwd), blk(bwd), blk(bwd), pl.BlockSpec((None, GB, LANE), bwd_g),
                  pl.BlockSpec((2, LANE, 2 * GDN_W), lambda b, i: (0, 0, 0)),
                  pl.BlockSpec((None, 2, LANE), lambda b, i: (layer, 0, 0))],
        out_specs=[blk(fwd), blk(bwd)],
        out_shape=[out, out],
        scratch_shapes=[pltpu.VMEM((2, GDN_W, GDN_W), F32)],
        compiler_params=pltpu.CompilerParams(vmem_limit_bytes=VMEM_BIG),
        name="gdn_scan",
    )(qn, kn, vn, p, qn, kn, vn, p, esel, gvec)


def _outproj_kernel(*refs, n_tok, skip):
    (h_ref, nb_ref, of_ref, ob_ref, zg_ref, m_ref, w_ref, nw_ref, hsum_ref, lng_ref, lnb_ref, o_ref) = refs[n_tok:]
    o = of_ref[...] + ob_ref[...]
    ms = _head_sum_sq(o, hsum_ref[...]) * (1.0 / HD)
    og = o * lax.rsqrt(ms + NORM_EPS) * nw_ref[...] * _silu(zg_ref[...])
    mixed = jnp.concatenate([h_ref[...].astype(BF16), nb_ref[...].astype(BF16), og.astype(BF16)], axis=1)
    y = _dot(mixed, w_ref[...])
    gate = m_ref[...][:, 2 * D:3 * D]
    z = DEEPNORM_ALPHA * _load_tokens(refs[:n_tok], pl.program_id(1) + skip) + gate * y
    mu = jnp.mean(z, axis=-1, keepdims=True)
    zc = z - mu
    var = jnp.mean(zc * zc, axis=-1, keepdims=True)
    o_ref[...] = zc * lax.rsqrt(var + LN_EPS) * lng_ref[...] + lnb_ref[...]


def _out_proj(tokens, hg, nb, o_f, o_b, p, modsel, w_out_b, nw4, hsum4, ln_g, ln_b, layer, latent_only):
    bsz, t = _token_shape(tokens)
    skip = LC // TM if latent_only else 0
    nt = t // TM - skip
    tok_specs, tok_ops = _token_specs(tokens, skip)
    row = lambda width: pl.BlockSpec((None, TM, width), lambda b, i: (b, i + skip, 0))
    vec = lambda: pl.BlockSpec((None, 1, D), lambda b, i: (layer, 0, 0))
    return pl.pallas_call(
        functools.partial(_outproj_kernel, n_tok=len(tok_ops), skip=skip),
        grid=(bsz, nt),
        in_specs=tok_specs + [
                  row(RG_W), row(NA_W), row(GDN_W), row(GDN_W),
                  pl.BlockSpec((None, TM, GDN_W), lambda b, i: (b, i + skip, C_ZG // GDN_W)),
                  pl.BlockSpec((None, None, 1, 3 * D),
                               lambda b, i: (layer, 2 * b + jnp.minimum(i + skip, 1), 0, 0)),
                  pl.BlockSpec((None, D, D), lambda b, i: (layer, 0, 0)),
                  pl.BlockSpec((None, 1, GDN_W), lambda b, i: (layer, 0, 0)),
                  pl.BlockSpec((GDN_W, GDN_W), lambda b, i: (0, 0)),
                  vec(), vec()],
        out_specs=pl.BlockSpec((None, TM, D), lambda b, i: (b, i, 0)),
        out_shape=jax.ShapeDtypeStruct((bsz, nt * TM, D), F32),
        compiler_params=pltpu.CompilerParams(vmem_limit_bytes=VMEM_BIG),
        name="out_proj",
    )(*tok_ops, hg, nb, o_f, o_b, p, modsel, w_out_b, nw4, hsum4, ln_g, ln_b)


def _na_bias_table(na_rpb):
    jq = np.arange(GRID_W)
    col_start = np.clip(jq - NA_COLS // 2, 0, GRID_W - NA_COLS)
    kc = np.arange(GRID_W)
    inside = (kc[None, :] >= col_start[:, None]) & (kc[None, :] < col_start[:, None] + NA_COLS)
    col_off = kc[None, :] - jq[:, None] + NA_COLS - 1
    onehot = (np.arange(2 * NA_COLS - 1)[:, None, None] == col_off[None]).astype(np.float32)
    toeplitz = jnp.einsum('lhrx,xjc->lhrjc', na_rpb, jnp.asarray(onehot), precision=HIGHEST)
    tab = jnp.where(inside[None, None, None], toeplitz, NEG)
    tab = jnp.concatenate([tab, jnp.full((DEPTH, NA_HEADS, 1, GRID_W, GRID_W), NEG, F32)], axis=2)
    return jnp.concatenate([tab, tab], axis=-1)


def _rope_tables(seq):
    pos = jnp.arange(seq)
    rows_pos, cols_pos = pos // GRID_W, pos % GRID_W
    half = HD // 2
    nf = half // 2
    inv_freq = ROPE_BASE ** (-jnp.arange(nf, dtype=F32) / nf)
    lane = np.arange(LANE)
    jl = lane % HD
    use_row = jl < half
    f = (jl % half) % nf
    sign = np.where((jl % half) < nf, -1.0, 1.0).astype(np.float32)
    pos_l = jnp.where(use_row[None, :], rows_pos[:, None], cols_pos[:, None]).astype(F32)
    ang = pos_l * inv_freq[f][None, :]
    cos_t = jnp.concatenate([jnp.ones((LC, LANE), F32), jnp.cos(ang)], axis=0)
    sin_t = jnp.concatenate([jnp.zeros((LC, LANE), F32), jnp.sin(ang) * sign[None, :]], axis=0)
    return cos_t, sin_t


def _gate_select():
    e = np.zeros((2, LANE, 2 * GDN_W), np.float32)
    for d in range(2):
        for l in range(GDN_W):
            h = l // HD
            e[d, d * GDN_HEADS + h, l] = 1.0
            e[d, 2 * GDN_HEADS + d * GDN_HEADS + h, GDN_W + l] = 1.0
    return jnp.asarray(e).astype(BF16)


def _head_sum(width):
    i = np.arange(width)
    return jnp.asarray((i[:, None] // HD == i[None, :] // HD).astype(np.float32)).astype(BF16)


def kernel(x, c, ctx, c_ctx, w_mod, b_mod, w_in, conv_w, rg_wa, rg_ba, rg_wx, rg_bx, rg_lam, na_rpb, gdn_alog,
           gdn_dtb, gdn_nw, w_out, ln_g, ln_b):
    bsz, seq, _ = x.shape
    assert ctx.shape[1] == LC and seq == GRID_W * GRID_W

    c8 = jnp.concatenate([c, c_ctx[None], jnp.zeros((SUB - bsz - 1, D), F32)], axis=0)
    w_in_p = _reorder_in_cols(w_in.astype(BF16))
    w_out_b = w_out.astype(BF16)
    ns = RG_W // LANE
    eye2 = jnp.eye(2, dtype=F32)

    def block_diag(wt):
        wt = wt.reshape(DEPTH, 2, ns, 2, RG_BLOCK, RG_BLOCK)
        return jnp.einsum('ldsjae,jk->ldsjake', wt, eye2).reshape(DEPTH, 2, ns, LANE, LANE)

    wa, wx = block_diag(rg_wa), block_diag(rg_wx)
    rg_w = jnp.concatenate([wa[:, 0], wx[:, 0], wa[:, 1], wx[:, 1]], axis=-1).astype(BF16)
    slab = lambda v: v.reshape(DEPTH, ns, 1, LANE)
    rg_b = jnp.concatenate([slab(rg_ba[:, 0]), slab(rg_bx[:, 0]), slab(rg_ba[:, 1]), slab(rg_bx[:, 1])], axis=-1)
    rg_lam2 = jnp.concatenate([slab(rg_lam[:, 0]), slab(rg_lam[:, 1])], axis=-1)
    conv_w3 = jnp.stack([conv_w[:, :, C_GQ:C_GK], conv_w[:, :, C_GK:C_GV], conv_w[:, :, C_GV:C_ZA]], axis=1)
    bias_tab = _na_bias_table(na_rpb)
    cos_t, sin_t = _rope_tables(seq)
    esel = _gate_select()
    pad = jnp.zeros((DEPTH, LANE - 4 * GDN_HEADS), F32)
    zero8 = jnp.zeros((DEPTH, 2 * GDN_HEADS), F32)
    gvec = jnp.stack([jnp.concatenate([zero8, gdn_alog.reshape(DEPTH, -1), pad], axis=-1),
                      jnp.concatenate([zero8, gdn_dtb.reshape(DEPTH, -1), pad], axis=-1)], axis=1)
    nw4 = jnp.tile(gdn_nw, (1, GDN_HEADS)).reshape(DEPTH, 1, GDN_W)
    hsum2, hsum4 = _head_sum(LANE), _head_sum(GDN_W)
    ln_g3, ln_b3 = ln_g.reshape(DEPTH, 1, D), ln_b.reshape(DEPTH, 1, D)

    mods = _modulation(c8, w_mod, b_mod)
    ctx_rows = jnp.broadcast_to(mods[:, bsz:bsz + 1], (DEPTH, bsz, 3 * D))
    modsel = jnp.stack([ctx_rows, mods[:, :bsz]], axis=2).reshape(DEPTH, 2 * bsz, 1, 3 * D)

    xa = (ctx, x)
    for layer in range(DEPTH):
        p, kv16 = _in_proj(xa, modsel, w_in_p, layer)
        hg = _rglru(p, conv_w, rg_w, rg_b, rg_lam2, layer)
        nb = _natten(p, kv16, bias_tab, layer)
        qn, kn, vn = _gdn_prep(p, conv_w3, cos_t, sin_t, hsum2, layer)
        o_f, o_b = _gdn_scan(qn, kn, vn, p, esel, gvec, layer)
        xa = _out_proj(xa, hg, nb, o_f, o_b, p, modsel, w_out_b, nw4, hsum4, ln_g3, ln_b3, layer,
                       latent_only=layer == DEPTH - 1)
    return xa
```

```python
import functools
import math

import numpy as np
import jax
import jax.numpy as jnp
from jax import lax
from jax.experimental import pallas as pl
from jax.experimental.pallas import tpu as pltpu

F32 = jnp.float32
BF16 = jnp.bfloat16
HIGHEST = lax.Precision.HIGHEST

D = 1024
DEPTH = 4
LC = 256
GRID_W = 64
CONV_K = 4
RG_W = 384
RG_BLOCK = 64
RG_C = 8.0
NA_HEADS = 6
NA_W = 384
NA_ROWS = 8
NA_COLS = 16
GDN_HEADS = 4
GDN_W = 256
HD = 64
CHUNK = 64
ROPE_BASE = 10000.0
DEEPNORM_ALPHA = (2.0 * DEPTH) ** 0.25
LN_EPS = 1e-5
NORM_EPS = 1e-6
NEG = -1e30

C_RGX = 0
C_GQ = 384
C_GK = 640
C_GV = 896
C_ZA = 1152
C_NQ = 1536
C_ZN = 1920
C_ZG = 2304
C_GATE = 2560
D_F32 = 2688
C_NK = 2688
C_NV = 3072
D_INP = 3456
N_GATE = 4 * GDN_HEADS


def _reorder_in_cols(w):
    nq_end = RG_W + 3 * GDN_W + RG_W + NA_W
    kv_end = nq_end + 2 * NA_W
    zg_end = kv_end + NA_W + GDN_W
    gates = w[..., zg_end:zg_end + N_GATE]
    pad = jnp.zeros(gates.shape[:-1] + (LANE - N_GATE,), w.dtype)
    return jnp.concatenate([w[..., :nq_end], w[..., kv_end:zg_end], gates, pad, w[..., nq_end:kv_end]], axis=-1)

LANE = 128
SUB = 8
TM = 256
RC = 128
VMEM_BIG = 56 * 1024 * 1024


def _sigmoid(x):
    return 0.5 * jnp.tanh(0.5 * x) + 0.5


def _silu(x):
    return x * _sigmoid(x)


def _softplus(x):
    return jnp.maximum(x, 0.0) + jnp.log1p(jnp.exp(-jnp.abs(x)))


def _dot(a, b, precision=None):
    return jnp.dot(a, b, preferred_element_type=F32, precision=precision)


def _split2(x):
    hi = x.astype(BF16)
    return hi, (x - hi.astype(F32)).astype(BF16)


def _dot_split(x, onehot):
    hi, lo = _split2(x)
    return _dot(hi, onehot) + _dot(lo, onehot)


def _dot_nt(a, b, precision=None):
    return lax.dot_general(a, b, (((1,), (1,)), ((), ())), preferred_element_type=F32, precision=precision)


def _dot_tn(a, b, precision=None):
    return lax.dot_general(a, b, (((0,), (0,)), ((), ())), preferred_element_type=F32, precision=precision)


def _mod_kernel(c_ref, w_ref, b_ref, o_ref):
    s = _silu(c_ref[...])
    o_ref[...] = _dot(s, w_ref[...], HIGHEST) + b_ref[...]


def _modulation(c8, w_mod, b_mod):
    nb = 3 * D // D
    return pl.pallas_call(
        _mod_kernel,
        grid=(DEPTH, nb),
        in_specs=[pl.BlockSpec((SUB, D), lambda l, j: (0, 0)),
                  pl.BlockSpec((None, D, D), lambda l, j: (l, 0, j)),
                  pl.BlockSpec((None, 1, D), lambda l, j: (l, 0, j))],
        out_specs=pl.BlockSpec((None, SUB, D), lambda l, j: (l, 0, j)),
        out_shape=jax.ShapeDtypeStruct((DEPTH, SUB, 3 * D), F32),
        name="modulation",
    )(c8, w_mod, b_mod.reshape(DEPTH, 1, 3 * D))


def _token_specs(tokens, skip=0):
    assert LC == TM
    if isinstance(tokens, tuple):
        return [pl.BlockSpec((None, LC, D), lambda b, i: (b, 0, 0)),
                pl.BlockSpec((None, TM, D), lambda b, i: (b, jnp.maximum(i + skip - 1, 0), 0))], list(tokens)
    return [pl.BlockSpec((None, TM, D), lambda b, i: (b, i + skip, 0))], [tokens]


def _load_tokens(refs, tile):
    if len(refs) == 2:
        return jnp.where(tile == 0, refs[0][...], refs[1][...])
    return refs[0][...]


def _token_shape(tokens):
    if isinstance(tokens, tuple):
        return tokens[1].shape[0], tokens[0].shape[1] + tokens[1].shape[1]
    return tokens.shape[0], tokens.shape[1]


def _inproj_kernel(*refs, n_tok):
    m_ref, w_ref, o_ref, kv_ref = refs[n_tok:]
    m = m_ref[...]
    shift = m[:, :D]
    scale = m[:, D:2 * D]
    u = _load_tokens(refs[:n_tok], pl.program_id(1)) * (1.0 + scale) + shift
    p = _dot(u.astype(BF16), w_ref[...])
    o_ref[...] = p[:, :D_F32]
    kv_ref[...] = p[:, C_NK:D_INP].astype(BF16)


def _in_proj(tokens, modsel, w_in_p, layer):
    bsz, t = _token_shape(tokens)
    nt = t // TM
    tok_specs, tok_ops = _token_specs(tokens)
    return pl.pallas_call(
        functools.partial(_inproj_kernel, n_tok=len(tok_ops)),
        grid=(bsz, nt),
        in_specs=tok_specs + [
                  pl.BlockSpec((None, None, 1, 3 * D), lambda b, i: (layer, 2 * b + jnp.minimum(i, 1), 0, 0)),
                  pl.BlockSpec((None, D, D_INP), lambda b, i: (layer, 0, 0))],
        out_specs=[pl.BlockSpec((None, TM, D_F32), lambda b, i: (b, i, 0)),
                   pl.BlockSpec((None, TM, 2 * NA_W), lambda b, i: (b, i, 0))],
        out_shape=[jax.ShapeDtypeStruct((bsz, t, D_F32), F32), jax.ShapeDtypeStruct((bsz, t, 2 * NA_W), BF16)],
        compiler_params=pltpu.CompilerParams(vmem_limit_bytes=VMEM_BIG),
        name="in_proj",
    )(*tok_ops, modsel, w_in_p)


PAD_ROWS = 3 * SUB


def _fill_padded(src_ref, pad_ref, t):
    zeros = jnp.zeros((SUB, LANE), F32)
    pad_ref[0:SUB, :] = zeros
    pad_ref[SUB + LC:2 * SUB + LC, :] = zeros
    pad_ref[t + 2 * SUB:t + 3 * SUB, :] = zeros

    def body(c, carry):
        src = pl.multiple_of(c * LC, LC)
        dst = pl.multiple_of(src + SUB + jnp.where(c >= 1, SUB, 0), SUB)
        pad_ref[pl.ds(dst, LC), :] = src_ref[pl.ds(src, LC), :]
        return carry

    lax.fori_loop(0, t // LC, body, 0)


def _conv_chunk(pad_ref, cw, t0):
    base = pl.multiple_of(t0 + jnp.where(t0 >= LC, SUB, 0), SUB) + SUB
    acc = pad_ref[pl.ds(base, RC), :] * cw[2:3, :]
    for k in (0, 1, 3):
        acc = acc + pad_ref[pl.ds(base + (k - CONV_K // 2), RC), :] * cw[k:k + 1, :]
    return acc


def _scan8(a, b, row, reverse):
    for s in (1, 2, 4):
        if reverse:
            sh, ok = SUB - s, row < SUB - s
        else:
            sh, ok = s, row >= s
        a_sh = jnp.where(ok, pltpu.roll(a, sh, 0), 1.0)
        b_sh = jnp.where(ok, pltpu.roll(b, sh, 0), 0.0)
        b = a * b_sh + b
        a = a * a_sh
    return a, b


def _rglru_kernel(x_ref, z_ref, cw_ref, w_ref, bias_ref, lam_ref, o_ref,
                  pad_ref, a0_ref, b0_ref, a1_ref, b1_ref, hf_ref, hb_ref):
    t = x_ref.shape[0]
    _fill_padded(x_ref, pad_ref, t)
    cw = cw_ref[...]
    w = w_ref[...]
    bias = bias_ref[...]
    ls = -_softplus(-lam_ref[...])

    def gates(c, carry):
        t0 = pl.multiple_of(c * RC, RC)
        u = _conv_chunk(pad_ref, cw, t0)
        g = _dot(u.astype(BF16), w) + bias
        for d, (a_ref, b_ref) in enumerate(((a0_ref, b0_ref), (a1_ref, b1_ref))):
            r = _sigmoid(g[:, (2 * d) * LANE:(2 * d + 1) * LANE])
            i = _sigmoid(g[:, (2 * d + 1) * LANE:(2 * d + 2) * LANE])
            log_a = RG_C * r * ls[:, d * LANE:(d + 1) * LANE]
            a = jnp.exp(log_a)
            mult = jnp.sqrt(-jnp.tanh(log_a) * (a * a + 1.0))
            a_ref[pl.ds(t0, RC), :] = a
            b_ref[pl.ds(t0, RC), :] = mult * (i * u)
        return carry

    lax.fori_loop(0, t // RC, gates, 0, unroll=2)

    row = lax.broadcasted_iota(jnp.int32, (SUB, LANE), 0)
    n_tiles = t // SUB
    n_ctx = LC // SUB

    def scan(i, carry):
        cf, cb = carry
        rf = pl.multiple_of(i * SUB, SUB)
        af, bf = _scan8(a0_ref[pl.ds(rf, SUB), :], b0_ref[pl.ds(rf, SUB), :], row, False)
        hf = bf + af * cf
        hf_ref[pl.ds(rf, SUB), :] = hf
        cf = jnp.broadcast_to(hf[SUB - 1:SUB, :], (SUB, LANE))
        j = jnp.where(i < n_ctx, n_ctx - 1 - i, n_tiles + n_ctx - 1 - i)
        rb = pl.multiple_of(j * SUB, SUB)
        ab, bb = _scan8(a1_ref[pl.ds(rb, SUB), :], b1_ref[pl.ds(rb, SUB), :], row, True)
        hb = bb + ab * cb
        hb_ref[pl.ds(rb, SUB), :] = hb
        cb = jnp.broadcast_to(hb[0:1, :], (SUB, LANE))
        return cf, cb

    zero = jnp.zeros((SUB, LANE), F32)
    lax.fori_loop(0, n_tiles, scan, (zero, zero), unroll=2)

    def finish(c, carry):
        t0 = pl.multiple_of(c * RC, RC)
        h = hf_ref[pl.ds(t0, RC), :] + hb_ref[pl.ds(t0, RC), :]
        o_ref[pl.ds(t0, RC), :] = h * _silu(z_ref[pl.ds(t0, RC), :])
        return carry

    lax.fori_loop(0, t // RC, finish, 0)


def _rglru(p, conv_w, rg_w, rg_b, rg_lam2, layer):
    bsz, t, _ = p.shape
    ns = RG_W // LANE
    seq = lambda: pltpu.VMEM((t, LANE), F32)
    return pl.pallas_call(
        _rglru_kernel,
        grid=(bsz, ns),
        in_specs=[pl.BlockSpec((None, t, LANE), lambda b, s: (b, 0, C_RGX // LANE + s)),
                  pl.BlockSpec((None, t, LANE), lambda b, s: (b, 0, C_ZA // LANE + s)),
                  pl.BlockSpec((None, CONV_K, LANE), lambda b, s: (layer, 0, C_RGX // LANE + s)),
                  pl.BlockSpec((None, None, LANE, 4 * LANE), lambda b, s: (layer, s, 0, 0)),
                  pl.BlockSpec((None, None, 1, 4 * LANE), lambda b, s: (layer, s, 0, 0)),
                  pl.BlockSpec((None, None, 1, 2 * LANE), lambda b, s: (layer, s, 0, 0))],
        out_specs=pl.BlockSpec((None, t, LANE), lambda b, s: (b, 0, s)),
        out_shape=jax.ShapeDtypeStruct((bsz, t, RG_W), F32),
        scratch_shapes=[pltpu.VMEM((t + PAD_ROWS, LANE), F32), seq(), seq(), seq(), seq(), seq(), seq()],
        compiler_params=pltpu.CompilerParams(vmem_limit_bytes=VMEM_BIG),
        name="rglru",
    )(p, p, conv_w, rg_w, rg_b, rg_lam2)


NA_QR = 2
NA_NQ = NA_QR * GRID_W
NA_GROUPS = 2
NA_BQ = NA_GROUPS * NA_NQ
NA_KR = 10
NA_MASKED = 2 * NA_ROWS - 1


def _na_kernel(q_ref, kv_ref, z_ref, tab_ref, o_ref):
    j = pl.program_id(1)
    n_ctx_steps = LC // NA_BQ
    ns = NA_W // LANE
    lane = lax.broadcasted_iota(jnp.int32, (NA_NQ, LANE), 1)
    first = lane < HD
    first_blk = lax.broadcasted_iota(jnp.int32, (GRID_W, LANE), 1) < HD
    chains = [(g, s) for g in range(NA_GROUPS) for s in range(ns)]

    def stacked_q(g, s):
        qs = q_ref[g * NA_NQ:(g + 1) * NA_NQ, s * LANE:(s + 1) * LANE] * (HD ** -0.5)
        return jnp.concatenate([jnp.where(first, qs, 0.0), jnp.where(first, 0.0, qs)], axis=0).astype(BF16)

    def finish(g, s, sc, vv):
        m = jnp.max(sc, axis=-1, keepdims=True)
        pr = jnp.exp(sc - m)
        den = jnp.sum(pr, axis=-1, keepdims=True)
        pv = _dot(pr.astype(BF16), vv) * (1.0 / den)
        o = jnp.where(first, pv[0:NA_NQ], pv[NA_NQ:2 * NA_NQ])
        rows, cols = slice(g * NA_NQ, (g + 1) * NA_NQ), slice(s * LANE, (s + 1) * LANE)
        o_ref[rows, cols] = o * _silu(z_ref[rows, cols])

    @pl.when(j >= n_ctx_steps)
    def _():
        n_loc = NA_KR * GRID_W
        base, idx = [], []
        for g in range(NA_GROUPS):
            r = NA_QR * (NA_GROUPS * (j - n_ctx_steps) + g)
            r0 = [jnp.clip(r + i - NA_ROWS // 2, 0, GRID_W - NA_ROWS) for i in range(NA_QR)]
            start = jnp.minimum(r0[0], GRID_W - NA_KR)
            base.append(pl.multiple_of(LC + start * GRID_W, GRID_W))
            idx.append([[jnp.where((start + wu >= r0[i]) & (start + wu < r0[i] + NA_ROWS),
                                   start + wu - (r + i) + NA_ROWS - 1, NA_MASKED) for wu in range(NA_KR)]
                        for i in range(NA_QR)])
        scores = {}
        for g, s in chains:
            kk = jnp.concatenate([kv_ref[pl.ds(base[g], n_loc), s * LANE:(s + 1) * LANE],
                                  kv_ref[0:LC, s * LANE:(s + 1) * LANE]], axis=0)
            scores[g, s] = _dot_nt(stacked_q(g, s), kk)
        for g, s in chains:
            rows = []
            for hh in range(2):
                h = 2 * s + hh
                for i in range(NA_QR):
                    rows.append(jnp.concatenate(
                        [jnp.where(first_blk, tab_ref[h, idx[g][i][2 * a]], tab_ref[h, idx[g][i][2 * a + 1]])
                         for a in range(NA_KR // 2)] + [jnp.zeros((GRID_W, LC), F32)], axis=1))
            scores[g, s] = scores[g, s] + jnp.concatenate(rows, axis=0)
        for g, s in chains:
            vv = jnp.concatenate([kv_ref[pl.ds(base[g], n_loc), NA_W + s * LANE:NA_W + (s + 1) * LANE],
                                  kv_ref[0:LC, NA_W + s * LANE:NA_W + (s + 1) * LANE]], axis=0)
            finish(g, s, scores[g, s], vv)

    @pl.when(j < n_ctx_steps)
    def _():
        scores = {(g, s): _dot_nt(stacked_q(g, s), kv_ref[0:LC, s * LANE:(s + 1) * LANE]) for g, s in chains}
        for g, s in chains:
            finish(g, s, scores[g, s], kv_ref[0:LC, NA_W + s * LANE:NA_W + (s + 1) * LANE])


def _natten(p, kv16, bias_tab, layer):
    bsz, t, _ = p.shape
    nq = t // NA_BQ
    return pl.pallas_call(
        _na_kernel,
        grid=(bsz, nq),
        in_specs=[pl.BlockSpec((None, NA_BQ, NA_W), lambda b, j: (b, j, C_NQ // NA_W)),
                  pl.BlockSpec((None, t, 2 * NA_W), lambda b, j: (b, 0, 0)),
                  pl.BlockSpec((None, NA_BQ, NA_W), lambda b, j: (b, j, C_ZN // NA_W)),
                  pl.BlockSpec((None, NA_HEADS, 2 * NA_ROWS, GRID_W, LANE), lambda b, j: (layer, 0, 0, 0, 0))],
        out_specs=pl.BlockSpec((None, NA_BQ, NA_W), lambda b, j: (b, j, 0)),
        out_shape=jax.ShapeDtypeStruct((bsz, t, NA_W), F32),
        compiler_params=pltpu.CompilerParams(vmem_limit_bytes=VMEM_BIG),
        name="natten",
    )(p, kv16, p, bias_tab)


def _swap16(x):
    lane = lax.broadcasted_iota(jnp.int32, x.shape, 1)
    return jnp.where(lane % 32 < 16, pltpu.roll(x, LANE - 16, 1), pltpu.roll(x, 16, 1))


def _head_sum_sq(x, hsum):
    return _dot_split(x * x, hsum)


def _gdn_prep_kernel(q_ref, k_ref, v_ref, cw_ref, cos_ref, sin_ref, hsum_ref, qo_ref, ko_ref, vo_ref,
                     qpad_ref, kpad_ref, vpad_ref):
    t = q_ref.shape[0]
    hsum = hsum_ref[...]
    streams = ((q_ref, qpad_ref, qo_ref, "q"), (k_ref, kpad_ref, ko_ref, "k"), (v_ref, vpad_ref, vo_ref, "v"))
    for src_ref, pad_ref, _, _ in streams:
        _fill_padded(src_ref, pad_ref, t)

    def body(c, carry):
        t0 = pl.multiple_of(c * RC, RC)
        xs = [_silu(_conv_chunk(pad_ref, cw_ref[n], t0)) for n, (_, pad_ref, _, _) in enumerate(streams)]
        ssq = [_head_sum_sq(x, hsum) for x in xs[:2]]
        for n, (_, _, dst_ref, mode) in enumerate(streams):
            x = xs[n]
            if mode != "v":
                x = x * lax.rsqrt(ssq[n] + NORM_EPS)
                x = x * cos_ref[pl.ds(t0, RC), :] + _swap16(x) * sin_ref[pl.ds(t0, RC), :]
            if mode == "q":
                x = x * (HD ** -0.5)
            dst_ref[pl.ds(t0, RC), :] = x
        return carry

    lax.fori_loop(0, t // RC, body, 0, unroll=2)


def _gdn_prep(p, conv_w3, cos_t, sin_t, hsum, layer):
    bsz, t, _ = p.shape
    ns = GDN_W // LANE
    col = lambda c0: (lambda b, s: (b, 0, c0 // LANE + s))
    out = jax.ShapeDtypeStruct((bsz, t, GDN_W), F32)
    return pl.pallas_call(
        _gdn_prep_kernel,
        grid=(bsz, ns),
        in_specs=[pl.BlockSpec((None, t, LANE), col(C_GQ)),
                  pl.BlockSpec((None, t, LANE), col(C_GK)),
                  pl.BlockSpec((None, t, LANE), col(C_GV)),
                  pl.BlockSpec((None, 3, CONV_K, LANE), lambda b, s: (layer, 0, 0, s)),
                  pl.BlockSpec((t, LANE), lambda b, s: (0, 0)),
                  pl.BlockSpec((t, LANE), lambda b, s: (0, 0)),
                  pl.BlockSpec((LANE, LANE), lambda b, s: (0, 0))],
        out_specs=[pl.BlockSpec((None, t, LANE), lambda b, s: (b, 0, s))] * 3,
        out_shape=[out, out, out],
        scratch_shapes=[pltpu.VMEM((t + PAD_ROWS, LANE), F32)] * 3,
        compiler_params=pltpu.CompilerParams(vmem_limit_bytes=VMEM_BIG),
        name="gdn_prep",
    )(p, p, p, conv_w3, cos_t, sin_t, hsum)


GB = 256
NCH = GB // CHUNK
GDN_NB = 4


def _gdn_scan_kernel(qf_ref, kf_ref, vf_ref, gf_ref, qb_ref, kb_ref, vb_ref, gb_ref,
                     esel_ref, gvec_ref, of_ref, ob_ref, s_ref):
    step = pl.program_id(1)
    w4 = GDN_W
    nb = qf_ref.shape[0]

    @pl.when(step == 0)
    def _():
        s_ref[...] = jnp.zeros_like(s_ref)

    row = lax.broadcasted_iota(jnp.int32, (CHUNK, w4), 0)
    jl = lax.broadcasted_iota(jnp.int32, (CHUNK, w4), 1) % HD
    r2 = lax.broadcasted_iota(jnp.int32, (w4, w4), 0)
    c2 = lax.broadcasted_iota(jnp.int32, (w4, w4), 1)
    bd = (r2 // HD) == (c2 // HD)
    eye = (jl == row).astype(F32)
    blk16 = (row // 16) == (jl // 16)
    blk32 = (row // 32) == (jl // 32)
    alog = gvec_ref[0:1, :]
    dtb = gvec_ref[1:2, :]
    glane = lax.broadcasted_iota(jnp.int32, (GB, LANE), 1)

    def expand(y):
        yb = y.astype(BF16)
        return jnp.where(bd, jnp.concatenate([yb, yb, yb, yb], axis=0), jnp.zeros((), BF16))

    def mm(x, y):
        return _dot(x.astype(BF16), expand(y))

    refs = ((qf_ref, kf_ref, vf_ref, gf_ref), (qb_ref, kb_ref, vb_ref, gb_ref))
    chains = []
    for n, d in [(n, d) for n in range(nb) for d in range(2)]:
        q_ref, k_ref, v_ref, g_ref = (r.at[n] for r in refs[d])
        raw = g_ref[...]
        comp = jnp.where(glane < 2 * GDN_HEADS, _sigmoid(raw), -jnp.exp(alog) * _softplus(raw + dtb))
        gexp = _dot_split(comp, esel_ref[d])
        if d == 0:
            incl, strict, upto, tri = jl <= row, jl < row, row <= jl, bd & (c2 <= r2)
        else:
            incl, strict, upto, tri = jl >= row, jl > row, row >= jl, bd & (c2 >= r2)
        last = CHUNK - 1 if d == 0 else 0
        g_all = gexp[:, w4:2 * w4]
        g_hi, g_lo = _split2(g_all)
        tri = tri.astype(BF16)
        gcum_all = _dot(tri, g_hi) + _dot(tri, g_lo)
        for c in range(NCH):
            rows = slice(c * CHUNK, (c + 1) * CHUNK)
            ch = dict(bi=n, d=d, c=c, q=q_ref[rows, :], k=k_ref[rows, :], v=v_ref[rows, :], beta=gexp[rows, 0:w4],
                      gcum=gcum_all[rows], incl=incl, strict=strict, last=last)
            grow = jnp.sum(jnp.where(upto, g_all[rows], 0.0), axis=0, keepdims=True)
            ch["decay"] = jnp.where(incl, jnp.exp(jnp.minimum(ch["gcum"] - grow, 0.0)), 0.0)
            ch["kbeta"] = ch["k"] * ch["beta"]
            chains.append(ch)

    for ch in chains:
        prod = _dot_nt(jnp.concatenate([ch["kbeta"], ch["q"]], axis=0).astype(BF16), expand(ch["k"]))
        ch["n"] = jnp.where(ch["strict"], prod[0:CHUNK] * ch["decay"], 0.0)
        ch["qk"] = jnp.where(ch["incl"], prod[CHUNK:2 * CHUNK] * ch["decay"], 0.0)
        n16 = jnp.where(blk16, ch["n"], 0.0)
        ch["t"] = eye - n16
        ch["pw"] = n16
    for ch in chains:
        ch["pw"] = mm(ch["pw"], ch["pw"])
    for _ in range(2):
        for ch in chains:
            both = mm(jnp.concatenate([ch["t"], ch["pw"]], axis=0), ch["pw"])
            ch["t"] = ch["t"] + both[0:CHUNK]
            ch["pw"] = both[CHUNK:2 * CHUNK]
    for ch in chains:
        ch["t"] = ch["t"] + mm(ch["t"], ch["pw"])
    for off in (jnp.where(blk32 & ~blk16, 1.0, 0.0), jnp.where(blk32, 0.0, 1.0)):
        for ch in chains:
            ch["te"] = mm(ch["t"], ch["n"] * off)
        for ch in chains:
            ch["t"] = ch["t"] - mm(ch["te"], ch["t"])
    for ch in chains:
        eg = jnp.exp(ch["gcum"])
        ch["u"] = mm(ch["t"], ch["v"] * ch["beta"])
        w = mm(ch["t"], ch["kbeta"] * eg)
        glast = ch["gcum"][ch["last"]:ch["last"] + 1, :]
        ch["wq"] = jnp.concatenate([w, ch["q"] * eg], axis=0).astype(BF16)
        ch["kg"] = (ch["k"] * jnp.exp(glast - ch["gcum"])).astype(BF16)
        ch["eglast"] = jnp.exp(glast)

    by_seq = {(ch["bi"], ch["d"], ch["c"]): ch for ch in chains}
    seqs = [(n, d) for n in range(nb) for d in range(2)]
    state = {sq: s_ref[2 * sq[0] + sq[1]] for sq in seqs}
    res = {}
    for i in range(NCH):
        cur = {sq: by_seq[sq[0], sq[1], i if sq[1] == 0 else NCH - 1 - i] for sq in seqs}
        ws = {sq: _dot(cur[sq]["wq"], state[sq].astype(BF16)) for sq in seqs}
        v_new = {sq: cur[sq]["u"] - ws[sq][0:CHUNK] for sq in seqs}
        for sq in seqs:
            res[sq + (cur[sq]["c"],)] = ws[sq][CHUNK:2 * CHUNK] + mm(cur[sq]["qk"], v_new[sq])
        for sq in seqs:
            upd = _dot_tn(cur[sq]["kg"], v_new[sq].astype(BF16))
            state[sq] = state[sq] * cur[sq]["eglast"] + jnp.where(bd, upd, 0.0)
    outs = (of_ref, ob_ref)
    for n, d in seqs:
        s_ref[2 * n + d] = state[n, d]
        for c in range(NCH):
            outs[d][n, c * CHUNK:(c + 1) * CHUNK, :] = res[n, d, c]


def _gdn_scan(qn, kn, vn, p, esel, gvec, layer):
    bsz, t, _ = p.shape
    nsteps = t // GB
    fwd = lambda b, i: (b, i, 0)
    bwd = lambda b, i: (b, jnp.where(i == 0, 0, nsteps - i), 0)
    gcol = C_GATE // LANE
    fwd_g = lambda b, i: (b, i, gcol)
    bwd_g = lambda b, i: (b, jnp.where(i == 0, 0, nsteps - i), gcol)
    nb = math.gcd(bsz, GDN_NB)
    blk = lambda im: pl.BlockSpec((nb, GB, GDN_W), im)
    out = jax.ShapeDtypeStruct((bsz, t, GDN_W), F32)
    return pl.pallas_call(
        _gdn_scan_kernel,
        grid=(bsz // nb, nsteps),
        in_specs=[blk(fwd), blk(fwd), blk(fwd), pl.BlockSpec((nb, GB, LANE), fwd_g),
                  blk(bwd), blk(bwd), blk(bwd), pl.BlockSpec((nb, GB, LANE), bwd_g),
                  pl.BlockSpec((2, LANE, 2 * GDN_W), lambda b, i: (0, 0, 0)),
                  pl.BlockSpec((None, 2, LANE), lambda b, i: (layer, 0, 0))],
        out_specs=[blk(fwd), blk(bwd)],
        out_shape=[out, out],
        scratch_shapes=[pltpu.VMEM((2 * nb, GDN_W, GDN_W), F32)],
        compiler_params=pltpu.CompilerParams(vmem_limit_bytes=VMEM_BIG),
        name="gdn_scan",
    )(qn, kn, vn, p, qn, kn, vn, p, esel, gvec)


def _outproj_kernel(*refs, n_tok, skip):
    (h_ref, nb_ref, of_ref, ob_ref, zg_ref, m_ref, w_ref, nw_ref, hsum_ref, lng_ref, lnb_ref, o_ref) = refs[n_tok:]
    o = of_ref[...] + ob_ref[...]
    ms = _head_sum_sq(o, hsum_ref[...]) * (1.0 / HD)
    og = o * lax.rsqrt(ms + NORM_EPS) * nw_ref[...] * _silu(zg_ref[...])
    mixed = jnp.concatenate([h_ref[...].astype(BF16), nb_ref[...].astype(BF16), og.astype(BF16)], axis=1)
    y = _dot(mixed, w_ref[...])
    gate = m_ref[...][:, 2 * D:3 * D]
    z = DEEPNORM_ALPHA * _load_tokens(refs[:n_tok], pl.program_id(1) + skip) + gate * y
    mu = jnp.mean(z, axis=-1, keepdims=True)
    zc = z - mu
    var = jnp.mean(zc * zc, axis=-1, keepdims=True)
    o_ref[...] = zc * lax.rsqrt(var + LN_EPS) * lng_ref[...] + lnb_ref[...]


def _out_proj(tokens, hg, nb, o_f, o_b, p, modsel, w_out_b, nw4, hsum4, ln_g, ln_b, layer, latent_only):
    bsz, t = _token_shape(tokens)
    skip = LC // TM if latent_only else 0
    nt = t // TM - skip
    tok_specs, tok_ops = _token_specs(tokens, skip)
    row = lambda width: pl.BlockSpec((None, TM, width), lambda b, i: (b, i + skip, 0))
    vec = lambda: pl.BlockSpec((None, 1, D), lambda b, i: (layer, 0, 0))
    return pl.pallas_call(
        functools.partial(_outproj_kernel, n_tok=len(tok_ops), skip=skip),
        grid=(bsz, nt),
        in_specs=tok_specs + [
                  row(RG_W), row(NA_W), row(GDN_W), row(GDN_W),
                  pl.BlockSpec((None, TM, GDN_W), lambda b, i: (b, i + skip, C_ZG // GDN_W)),
                  pl.BlockSpec((None, None, 1, 3 * D),
                               lambda b, i: (layer, 2 * b + jnp.minimum(i + skip, 1), 0, 0)),
                  pl.BlockSpec((None, D, D), lambda b, i: (layer, 0, 0)),
                  pl.BlockSpec((None, 1, GDN_W), lambda b, i: (layer, 0, 0)),
                  pl.BlockSpec((GDN_W, GDN_W), lambda b, i: (0, 0)),
                  vec(), vec()],
        out_specs=pl.BlockSpec((None, TM, D), lambda b, i: (b, i, 0)),
        out_shape=jax.ShapeDtypeStruct((bsz, nt * TM, D), F32),
        compiler_params=pltpu.CompilerParams(vmem_limit_bytes=VMEM_BIG),
        name="out_proj",
    )(*tok_ops, hg, nb, o_f, o_b, p, modsel, w_out_b, nw4, hsum4, ln_g, ln_b)


def _na_bias_table(na_rpb):
    jq = np.arange(GRID_W)
    col_start = np.clip(jq - NA_COLS // 2, 0, GRID_W - NA_COLS)
    kc = np.arange(GRID_W)
    inside = (kc[None, :] >= col_start[:, None]) & (kc[None, :] < col_start[:, None] + NA_COLS)
    col_off = kc[None, :] - jq[:, None] + NA_COLS - 1
    onehot = (np.arange(2 * NA_COLS - 1)[:, None, None] == col_off[None]).astype(np.float32)
    toeplitz = jnp.einsum('lhrx,xjc->lhrjc', na_rpb, jnp.asarray(onehot), precision=HIGHEST)
    tab = jnp.where(inside[None, None, None], toeplitz, NEG)
    tab = jnp.concatenate([tab, jnp.full((DEPTH, NA_HEADS, 1, GRID_W, GRID_W), NEG, F32)], axis=2)
    return jnp.concatenate([tab, tab], axis=-1)


def _rope_tables(seq):
    pos = jnp.arange(seq)
    rows_pos, cols_pos = pos // GRID_W, pos % GRID_W
    half = HD // 2
    nf = half // 2
    inv_freq = ROPE_BASE ** (-jnp.arange(nf, dtype=F32) / nf)
    lane = np.arange(LANE)
    jl = lane % HD
    use_row = jl < half
    f = (jl % half) % nf
    sign = np.where((jl % half) < nf, -1.0, 1.0).astype(np.float32)
    pos_l = jnp.where(use_row[None, :], rows_pos[:, None], cols_pos[:, None]).astype(F32)
    ang = pos_l * inv_freq[f][None, :]
    cos_t = jnp.concatenate([jnp.ones((LC, LANE), F32), jnp.cos(ang)], axis=0)
    sin_t = jnp.concatenate([jnp.zeros((LC, LANE), F32), jnp.sin(ang) * sign[None, :]], axis=0)
    return cos_t, sin_t


def _gate_select():
    e = np.zeros((2, LANE, 2 * GDN_W), np.float32)
    for d in range(2):
        for l in range(GDN_W):
            h = l // HD
            e[d, d * GDN_HEADS + h, l] = 1.0
            e[d, 2 * GDN_HEADS + d * GDN_HEADS + h, GDN_W + l] = 1.0
    return jnp.asarray(e).astype(BF16)


def _head_sum(width):
    i = np.arange(width)
    return jnp.asarray((i[:, None] // HD == i[None, :] // HD).astype(np.float32)).astype(BF16)


def kernel(x, c, ctx, c_ctx, w_mod, b_mod, w_in, conv_w, rg_wa, rg_ba, rg_wx, rg_bx, rg_lam, na_rpb, gdn_alog,
           gdn_dtb, gdn_nw, w_out, ln_g, ln_b):
    bsz, seq, _ = x.shape
    assert ctx.shape[1] == LC and seq == GRID_W * GRID_W

    c8 = jnp.concatenate([c, c_ctx[None], jnp.zeros((SUB - bsz - 1, D), F32)], axis=0)
    w_in_p = _reorder_in_cols(w_in.astype(BF16))
    w_out_b = w_out.astype(BF16)
    ns = RG_W // LANE
    eye2 = jnp.eye(2, dtype=F32)

    def block_diag(wt):
        wt = wt.reshape(DEPTH, 2, ns, 2, RG_BLOCK, RG_BLOCK)
        return jnp.einsum('ldsjae,jk->ldsjake', wt, eye2).reshape(DEPTH, 2, ns, LANE, LANE)

    wa, wx = block_diag(rg_wa), block_diag(rg_wx)
    rg_w = jnp.concatenate([wa[:, 0], wx[:, 0], wa[:, 1], wx[:, 1]], axis=-1).astype(BF16)
    slab = lambda v: v.reshape(DEPTH, ns, 1, LANE)
    rg_b = jnp.concatenate([slab(rg_ba[:, 0]), slab(rg_bx[:, 0]), slab(rg_ba[:, 1]), slab(rg_bx[:, 1])], axis=-1)
    rg_lam2 = jnp.concatenate([slab(rg_lam[:, 0]), slab(rg_lam[:, 1])], axis=-1)
    conv_w3 = jnp.stack([conv_w[:, :, C_GQ:C_GK], conv_w[:, :, C_GK:C_GV], conv_w[:, :, C_GV:C_ZA]], axis=1)
    bias_tab = _na_bias_table(na_rpb)
    cos_t, sin_t = _rope_tables(seq)
    esel = _gate_select()
    pad = jnp.zeros((DEPTH, LANE - 4 * GDN_HEADS), F32)
    zero8 = jnp.zeros((DEPTH, 2 * GDN_HEADS), F32)
    gvec = jnp.stack([jnp.concatenate([zero8, gdn_alog.reshape(DEPTH, -1), pad], axis=-1),
                      jnp.concatenate([zero8, gdn_dtb.reshape(DEPTH, -1), pad], axis=-1)], axis=1)
    nw4 = jnp.tile(gdn_nw, (1, GDN_HEADS)).reshape(DEPTH, 1, GDN_W)
    hsum2, hsum4 = _head_sum(LANE), _head_sum(GDN_W)
    ln_g3, ln_b3 = ln_g.reshape(DEPTH, 1, D), ln_b.reshape(DEPTH, 1, D)

    mods = _modulation(c8, w_mod, b_mod)
    ctx_rows = jnp.broadcast_to(mods[:, bsz:bsz + 1], (DEPTH, bsz, 3 * D))
    modsel = jnp.stack([ctx_rows, mods[:, :bsz]], axis=2).reshape(DEPTH, 2 * bsz, 1, 3 * D)

    xa = (ctx, x)
    for layer in range(DEPTH):
        p, kv16 = _in_proj(xa, modsel, w_in_p, layer)
        hg = _rglru(p, conv_w, rg_w, rg_b, rg_lam2, layer)
        nb = _natten(p, kv16, bias_tab, layer)
        qn, kn, vn = _gdn_prep(p, conv_w3, cos_t, sin_t, hsum2, layer)
        o_f, o_b = _gdn_scan(qn, kn, vn, p, esel, gvec, layer)
        xa = _out_proj(xa, hg, nb, o_f, o_b, p, modsel, w_out_b, nw4, hsum4, ln_g3, ln_b3, layer,
                       latent_only=layer == DEPTH - 1)
    return xa
```

```python
import functools
import math

import numpy as np
import jax
import jax.numpy as jnp
from jax import lax
from jax.experimental import pallas as pl
from jax.experimental.pallas import tpu as pltpu

F32 = jnp.float32
BF16 = jnp.bfloat16
HIGHEST = lax.Precision.HIGHEST

D = 1024
DEPTH = 4
LC = 256
GRID_W = 64
CONV_K = 4
RG_W = 384
RG_BLOCK = 64
RG_C = 8.0
NA_HEADS = 6
NA_W = 384
NA_ROWS = 8
NA_COLS = 16
GDN_HEADS = 4
GDN_W = 256
HD = 64
CHUNK = 64
ROPE_BASE = 10000.0
DEEPNORM_ALPHA = (2.0 * DEPTH) ** 0.25
LN_EPS = 1e-5
NORM_EPS = 1e-6
NEG = -1e30

C_RGX = 0
C_GQ = 384
C_GK = 640
C_GV = 896
C_ZA = 1152
C_NQ = 1536
C_ZN = 1920
C_ZG = 2304
C_GATE = 2560
D_F32 = 2688
C_NK = 2688
C_NV = 3072
D_INP = 3456
N_GATE = 4 * GDN_HEADS


def _reorder_in_cols(w):
    nq_end = RG_W + 3 * GDN_W + RG_W + NA_W
    kv_end = nq_end + 2 * NA_W
    zg_end = kv_end + NA_W + GDN_W
    gates = w[..., zg_end:zg_end + N_GATE]
    pad = jnp.zeros(gates.shape[:-1] + (LANE - N_GATE,), w.dtype)
    return jnp.concatenate([w[..., :nq_end], w[..., kv_end:zg_end], gates, pad, w[..., nq_end:kv_end]], axis=-1)

LANE = 128
SUB = 8
TM = 256
RC = 128
VMEM_BIG = 56 * 1024 * 1024


def _sigmoid(x):
    return 0.5 * jnp.tanh(0.5 * x) + 0.5


def _silu(x):
    return x * _sigmoid(x)


def _softplus(x):
    return jnp.maximum(x, 0.0) + jnp.log1p(jnp.exp(-jnp.abs(x)))


def _dot(a, b, precision=None):
    return jnp.dot(a, b, preferred_element_type=F32, precision=precision)


def _split2(x):
    hi = x.astype(BF16)
    return hi, (x - hi.astype(F32)).astype(BF16)


def _dot_split(x, onehot):
    hi, lo = _split2(x)
    return _dot(hi, onehot) + _dot(lo, onehot)


def _dot_nt(a, b, precision=None):
    return lax.dot_general(a, b, (((1,), (1,)), ((), ())), preferred_element_type=F32, precision=precision)


def _dot_tn(a, b, precision=None):
    return lax.dot_general(a, b, (((0,), (0,)), ((), ())), preferred_element_type=F32, precision=precision)


def _mod_kernel(c_ref, w_ref, b_ref, o_ref):
    s = _silu(c_ref[...])
    o_ref[...] = _dot(s, w_ref[...], HIGHEST) + b_ref[...]


def _modulation(c8, w_mod, b_mod):
    nb = 3 * D // D
    return pl.pallas_call(
        _mod_kernel,
        grid=(DEPTH, nb),
        in_specs=[pl.BlockSpec((SUB, D), lambda l, j: (0, 0)),
                  pl.BlockSpec((None, D, D), lambda l, j: (l, 0, j)),
                  pl.BlockSpec((None, 1, D), lambda l, j: (l, 0, j))],
        out_specs=pl.BlockSpec((None, SUB, D), lambda l, j: (l, 0, j)),
        out_shape=jax.ShapeDtypeStruct((DEPTH, SUB, 3 * D), F32),
        name="modulation",
    )(c8, w_mod, b_mod.reshape(DEPTH, 1, 3 * D))


def _token_specs(tokens, skip=0):
    assert LC == TM
    if isinstance(tokens, tuple):
        return [pl.BlockSpec((None, LC, D), lambda b, i: (b, 0, 0)),
                pl.BlockSpec((None, TM, D), lambda b, i: (b, jnp.maximum(i + skip - 1, 0), 0))], list(tokens)
    return [pl.BlockSpec((None, TM, D), lambda b, i: (b, i + skip, 0))], [tokens]


def _load_tokens(refs, tile):
    if len(refs) == 2:
        return jnp.where(tile == 0, refs[0][...], refs[1][...])
    return refs[0][...]


def _token_shape(tokens):
    if isinstance(tokens, tuple):
        return tokens[1].shape[0], tokens[0].shape[1] + tokens[1].shape[1]
    return tokens.shape[0], tokens.shape[1]


def _inproj_kernel(*refs, n_tok):
    m_ref, w_ref, o_ref, kv_ref = refs[n_tok:]
    m = m_ref[...]
    shift = m[:, :D]
    scale = m[:, D:2 * D]
    u = _load_tokens(refs[:n_tok], pl.program_id(1)) * (1.0 + scale) + shift
    p = _dot(u.astype(BF16), w_ref[...])
    o_ref[...] = p[:, :D_F32]
    kv_ref[...] = p[:, C_NK:D_INP].astype(BF16)


def _in_proj(tokens, modsel, w_in_p, layer):
    bsz, t = _token_shape(tokens)
    nt = t // TM
    tok_specs, tok_ops = _token_specs(tokens)
    return pl.pallas_call(
        functools.partial(_inproj_kernel, n_tok=len(tok_ops)),
        grid=(bsz, nt),
        in_specs=tok_specs + [
                  pl.BlockSpec((None, None, 1, 3 * D), lambda b, i: (layer, 2 * b + jnp.minimum(i, 1), 0, 0)),
                  pl.BlockSpec((None, D, D_INP), lambda b, i: (layer, 0, 0))],
        out_specs=[pl.BlockSpec((None, TM, D_F32), lambda b, i: (b, i, 0)),
                   pl.BlockSpec((None, TM, 2 * NA_W), lambda b, i: (b, i, 0))],
        out_shape=[jax.ShapeDtypeStruct((bsz, t, D_F32), F32), jax.ShapeDtypeStruct((bsz, t, 2 * NA_W), BF16)],
        compiler_params=pltpu.CompilerParams(vmem_limit_bytes=VMEM_BIG),
        name="in_proj",
    )(*tok_ops, modsel, w_in_p)


PAD_ROWS = 3 * SUB


def _fill_padded(src_ref, pad_ref, t):
    zeros = jnp.zeros((SUB, LANE), F32)
    pad_ref[0:SUB, :] = zeros
    pad_ref[SUB + LC:2 * SUB + LC, :] = zeros
    pad_ref[t + 2 * SUB:t + 3 * SUB, :] = zeros

    def body(c, carry):
        src = pl.multiple_of(c * LC, LC)
        dst = pl.multiple_of(src + SUB + jnp.where(c >= 1, SUB, 0), SUB)
        pad_ref[pl.ds(dst, LC), :] = src_ref[pl.ds(src, LC), :]
        return carry

    lax.fori_loop(0, t // LC, body, 0)


def _chunk_loop(n, body, unroll):
    main = n - n % unroll
    lax.fori_loop(0, main, body, 0, unroll=unroll)
    if main < n:
        lax.fori_loop(main, n, body, 0, unroll=True)


def _conv_chunk(pad_ref, cw, t0):
    base = pl.multiple_of(t0 + jnp.where(t0 >= LC, SUB, 0), SUB) + SUB
    acc = pad_ref[pl.ds(base, RC), :] * cw[2:3, :]
    for k in (0, 1, 3):
        acc = acc + pad_ref[pl.ds(base + (k - CONV_K // 2), RC), :] * cw[k:k + 1, :]
    return acc


def _scan8(a, b, row, reverse):
    for s in (1, 2, 4):
        if reverse:
            sh, ok = SUB - s, row < SUB - s
        else:
            sh, ok = s, row >= s
        a_sh = jnp.where(ok, pltpu.roll(a, sh, 0), 1.0)
        b_sh = jnp.where(ok, pltpu.roll(b, sh, 0), 0.0)
        b = a * b_sh + b
        a = a * a_sh
    return a, b


def _rglru_kernel(x_ref, z_ref, cw_ref, w_ref, bias_ref, lam_ref, o_ref,
                  pad_ref, a0_ref, b0_ref, a1_ref, b1_ref, hf_ref, hb_ref):
    t = x_ref.shape[0]
    _fill_padded(x_ref, pad_ref, t)
    cw = cw_ref[...]
    w = w_ref[...]
    bias = bias_ref[...]
    ls = -_softplus(-lam_ref[...])

    def gates(c, carry):
        t0 = pl.multiple_of(c * RC, RC)
        u = _conv_chunk(pad_ref, cw, t0)
        g = _dot(u.astype(BF16), w) + bias
        for d, (a_ref, b_ref) in enumerate(((a0_ref, b0_ref), (a1_ref, b1_ref))):
            r = _sigmoid(g[:, (2 * d) * LANE:(2 * d + 1) * LANE])
            i = _sigmoid(g[:, (2 * d + 1) * LANE:(2 * d + 2) * LANE])
            log_a = RG_C * r * ls[:, d * LANE:(d + 1) * LANE]
            a = jnp.exp(log_a)
            mult = jnp.sqrt(-jnp.tanh(log_a) * (a * a + 1.0))
            a_ref[pl.ds(t0, RC), :] = a
            b_ref[pl.ds(t0, RC), :] = mult * (i * u)
        return carry

    _chunk_loop(t // RC, gates, 8)

    row = lax.broadcasted_iota(jnp.int32, (SUB, LANE), 0)
    n_tiles = t // SUB
    n_ctx = LC // SUB

    def scan(i, carry):
        cf, cb = carry
        rf = pl.multiple_of(i * SUB, SUB)
        af, bf = _scan8(a0_ref[pl.ds(rf, SUB), :], b0_ref[pl.ds(rf, SUB), :], row, False)
        hf = bf + af * cf
        hf_ref[pl.ds(rf, SUB), :] = hf
        cf = jnp.broadcast_to(hf[SUB - 1:SUB, :], (SUB, LANE))
        j = jnp.where(i < n_ctx, n_ctx - 1 - i, n_tiles + n_ctx - 1 - i)
        rb = pl.multiple_of(j * SUB, SUB)
        ab, bb = _scan8(a1_ref[pl.ds(rb, SUB), :], b1_ref[pl.ds(rb, SUB), :], row, True)
        hb = bb + ab * cb
        hb_ref[pl.ds(rb, SUB), :] = hb
        cb = jnp.broadcast_to(hb[0:1, :], (SUB, LANE))
        return cf, cb

    zero = jnp.zeros((SUB, LANE), F32)
    lax.fori_loop(0, n_tiles, scan, (zero, zero), unroll=8)

    def finish(c, carry):
        t0 = pl.multiple_of(c * RC, RC)
        h = hf_ref[pl.ds(t0, RC), :] + hb_ref[pl.ds(t0, RC), :]
        o_ref[pl.ds(t0, RC), :] = h * _silu(z_ref[pl.ds(t0, RC), :])
        return carry

    _chunk_loop(t // RC, finish, 8)


def _rglru(p, conv_w, rg_w, rg_b, rg_lam2, layer):
    bsz, t, _ = p.shape
    ns = RG_W // LANE
    seq = lambda: pltpu.VMEM((t, LANE), F32)
    return pl.pallas_call(
        _rglru_kernel,
        grid=(bsz, ns),
        in_specs=[pl.BlockSpec((None, t, LANE), lambda b, s: (b, 0, C_RGX // LANE + s)),
                  pl.BlockSpec((None, t, LANE), lambda b, s: (b, 0, C_ZA // LANE + s)),
                  pl.BlockSpec((None, CONV_K, LANE), lambda b, s: (layer, 0, C_RGX // LANE + s)),
                  pl.BlockSpec((None, None, LANE, 4 * LANE), lambda b, s: (layer, s, 0, 0)),
                  pl.BlockSpec((None, None, 1, 4 * LANE), lambda b, s: (layer, s, 0, 0)),
                  pl.BlockSpec((None, None, 1, 2 * LANE), lambda b, s: (layer, s, 0, 0))],
        out_specs=pl.BlockSpec((None, t, LANE), lambda b, s: (b, 0, s)),
        out_shape=jax.ShapeDtypeStruct((bsz, t, RG_W), F32),
        scratch_shapes=[pltpu.VMEM((t + PAD_ROWS, LANE), F32), seq(), seq(), seq(), seq(), seq(), seq()],
        compiler_params=pltpu.CompilerParams(vmem_limit_bytes=VMEM_BIG),
        name="rglru",
    )(p, p, conv_w, rg_w, rg_b, rg_lam2)


NA_QR = 2
NA_NQ = NA_QR * GRID_W
NA_GROUPS = 2
NA_BQ = NA_GROUPS * NA_NQ
NA_KR = 10
NA_MASKED = 2 * NA_ROWS - 1


def _na_kernel(q_ref, kv_ref, z_ref, tab_ref, o_ref):
    j = pl.program_id(1)
    n_ctx_steps = LC // NA_BQ
    ns = NA_W // LANE
    lane = lax.broadcasted_iota(jnp.int32, (NA_NQ, LANE), 1)
    first = lane < HD
    first_blk = lax.broadcasted_iota(jnp.int32, (GRID_W, LANE), 1) < HD
    chains = [(g, s) for g in range(NA_GROUPS) for s in range(ns)]

    def stacked_q(g, s):
        qs = q_ref[g * NA_NQ:(g + 1) * NA_NQ, s * LANE:(s + 1) * LANE] * (HD ** -0.5)
        return jnp.concatenate([jnp.where(first, qs, 0.0), jnp.where(first, 0.0, qs)], axis=0).astype(BF16)

    def finish(g, s, sc, vv):
        m = jnp.max(sc, axis=-1, keepdims=True)
        pr = jnp.exp(sc - m)
        den = jnp.sum(pr, axis=-1, keepdims=True)
        pv = _dot(pr.astype(BF16), vv) * (1.0 / den)
        o = jnp.where(first, pv[0:NA_NQ], pv[NA_NQ:2 * NA_NQ])
        rows, cols = slice(g * NA_NQ, (g + 1) * NA_NQ), slice(s * LANE, (s + 1) * LANE)
        o_ref[rows, cols] = o * _silu(z_ref[rows, cols])

    @pl.when(j >= n_ctx_steps)
    def _():
        n_loc = NA_KR * GRID_W
        base, idx = [], []
        for g in range(NA_GROUPS):
            r = NA_QR * (NA_GROUPS * (j - n_ctx_steps) + g)
            r0 = [jnp.clip(r + i - NA_ROWS // 2, 0, GRID_W - NA_ROWS) for i in range(NA_QR)]
            start = jnp.minimum(r0[0], GRID_W - NA_KR)
            base.append(pl.multiple_of(LC + start * GRID_W, GRID_W))
            idx.append([[jnp.where((start + wu >= r0[i]) & (start + wu < r0[i] + NA_ROWS),
                                   start + wu - (r + i) + NA_ROWS - 1, NA_MASKED) for wu in range(NA_KR)]
                        for i in range(NA_QR)])
        scores = {}
        for g, s in chains:
            kk = jnp.concatenate([kv_ref[pl.ds(base[g], n_loc), s * LANE:(s + 1) * LANE],
                                  kv_ref[0:LC, s * LANE:(s + 1) * LANE]], axis=0)
            scores[g, s] = _dot_nt(stacked_q(g, s), kk)
        for g, s in chains:
            rows = []
            for hh in range(2):
                h = 2 * s + hh
                for i in range(NA_QR):
                    rows.append(jnp.concatenate(
                        [jnp.where(first_blk, tab_ref[h, idx[g][i][2 * a]], tab_ref[h, idx[g][i][2 * a + 1]])
                         for a in range(NA_KR // 2)] + [jnp.zeros((GRID_W, LC), F32)], axis=1))
            scores[g, s] = scores[g, s] + jnp.concatenate(rows, axis=0)
        for g, s in chains:
            vv = jnp.concatenate([kv_ref[pl.ds(base[g], n_loc), NA_W + s * LANE:NA_W + (s + 1) * LANE],
                                  kv_ref[0:LC, NA_W + s * LANE:NA_W + (s + 1) * LANE]], axis=0)
            finish(g, s, scores[g, s], vv)

    @pl.when(j < n_ctx_steps)
    def _():
        scores = {(g, s): _dot_nt(stacked_q(g, s), kv_ref[0:LC, s * LANE:(s + 1) * LANE]) for g, s in chains}
        for g, s in chains:
            finish(g, s, scores[g, s], kv_ref[0:LC, NA_W + s * LANE:NA_W + (s + 1) * LANE])


def _natten(p, kv16, bias_tab, layer):
    bsz, t, _ = p.shape
    nq = t // NA_BQ
    return pl.pallas_call(
        _na_kernel,
        grid=(bsz, nq),
        in_specs=[pl.BlockSpec((None, NA_BQ, NA_W), lambda b, j: (b, j, C_NQ // NA_W)),
                  pl.BlockSpec((None, t, 2 * NA_W), lambda b, j: (b, 0, 0)),
                  pl.BlockSpec((None, NA_BQ, NA_W), lambda b, j: (b, j, C_ZN // NA_W)),
                  pl.BlockSpec((None, NA_HEADS, 2 * NA_ROWS, GRID_W, LANE), lambda b, j: (layer, 0, 0, 0, 0))],
        out_specs=pl.BlockSpec((None, NA_BQ, NA_W), lambda b, j: (b, j, 0)),
        out_shape=jax.ShapeDtypeStruct((bsz, t, NA_W), F32),
        compiler_params=pltpu.CompilerParams(vmem_limit_bytes=VMEM_BIG),
        name="natten",
    )(p, kv16, p, bias_tab)


def _swap16(x):
    lane = lax.broadcasted_iota(jnp.int32, x.shape, 1)
    return jnp.where(lane % 32 < 16, pltpu.roll(x, LANE - 16, 1), pltpu.roll(x, 16, 1))


def _head_sum_sq(x, hsum):
    return _dot_split(x * x, hsum)


def _gdn_prep_kernel(q_ref, k_ref, v_ref, cw_ref, cos_ref, sin_ref, hsum_ref, qo_ref, ko_ref, vo_ref,
                     qpad_ref, kpad_ref, vpad_ref):
    t = q_ref.shape[0]
    hsum = hsum_ref[...]
    streams = ((q_ref, qpad_ref, qo_ref, "q"), (k_ref, kpad_ref, ko_ref, "k"), (v_ref, vpad_ref, vo_ref, "v"))
    for src_ref, pad_ref, _, _ in streams:
        _fill_padded(src_ref, pad_ref, t)

    def body(c, carry):
        t0 = pl.multiple_of(c * RC, RC)
        xs = [_silu(_conv_chunk(pad_ref, cw_ref[n], t0)) for n, (_, pad_ref, _, _) in enumerate(streams)]
        ssq = [_head_sum_sq(x, hsum) for x in xs[:2]]
        for n, (_, _, dst_ref, mode) in enumerate(streams):
            x = xs[n]
            if mode != "v":
                x = x * lax.rsqrt(ssq[n] + NORM_EPS)
                x = x * cos_ref[pl.ds(t0, RC), :] + _swap16(x) * sin_ref[pl.ds(t0, RC), :]
            if mode == "q":
                x = x * (HD ** -0.5)
            dst_ref[pl.ds(t0, RC), :] = x
        return carry

    _chunk_loop(t // RC, body, 4)


def _gdn_prep(p, conv_w3, cos_t, sin_t, hsum, layer):
    bsz, t, _ = p.shape
    ns = GDN_W // LANE
    col = lambda c0: (lambda b, s: (b, 0, c0 // LANE + s))
    out = jax.ShapeDtypeStruct((bsz, t, GDN_W), F32)
    return pl.pallas_call(
        _gdn_prep_kernel,
        grid=(bsz, ns),
        in_specs=[pl.BlockSpec((None, t, LANE), col(C_GQ)),
                  pl.BlockSpec((None, t, LANE), col(C_GK)),
                  pl.BlockSpec((None, t, LANE), col(C_GV)),
                  pl.BlockSpec((None, 3, CONV_K, LANE), lambda b, s: (layer, 0, 0, s)),
                  pl.BlockSpec((t, LANE), lambda b, s: (0, 0)),
                  pl.BlockSpec((t, LANE), lambda b, s: (0, 0)),
                  pl.BlockSpec((LANE, LANE), lambda b, s: (0, 0))],
        out_specs=[pl.BlockSpec((None, t, LANE), lambda b, s: (b, 0, s))] * 3,
        out_shape=[out, out, out],
        scratch_shapes=[pltpu.VMEM((t + PAD_ROWS, LANE), F32)] * 3,
        compiler_params=pltpu.CompilerParams(vmem_limit_bytes=VMEM_BIG),
        name="gdn_prep",
    )(p, p, p, conv_w3, cos_t, sin_t, hsum)


GB = 256
NCH = GB // CHUNK
GDN_NB = 4


def _gdn_scan_kernel(qf_ref, kf_ref, vf_ref, gf_ref, qb_ref, kb_ref, vb_ref, gb_ref,
                     esel_ref, gvec_ref, of_ref, ob_ref, s_ref):
    step = pl.program_id(1)
    w4 = GDN_W
    nb = qf_ref.shape[0]

    @pl.when(step == 0)
    def _():
        s_ref[...] = jnp.zeros_like(s_ref)

    row = lax.broadcasted_iota(jnp.int32, (CHUNK, w4), 0)
    jl = lax.broadcasted_iota(jnp.int32, (CHUNK, w4), 1) % HD
    r2 = lax.broadcasted_iota(jnp.int32, (w4, w4), 0)
    c2 = lax.broadcasted_iota(jnp.int32, (w4, w4), 1)
    bd = (r2 // HD) == (c2 // HD)
    eye = (jl == row).astype(F32)
    blk16 = (row // 16) == (jl // 16)
    blk32 = (row // 32) == (jl // 32)
    alog = gvec_ref[0:1, :]
    dtb = gvec_ref[1:2, :]
    glane = lax.broadcasted_iota(jnp.int32, (GB, LANE), 1)

    def expand(y):
        yb = y.astype(BF16)
        return jnp.where(bd, jnp.concatenate([yb, yb, yb, yb], axis=0), jnp.zeros((), BF16))

    def mm(x, y):
        return _dot(x.astype(BF16), expand(y))

    refs = ((qf_ref, kf_ref, vf_ref, gf_ref), (qb_ref, kb_ref, vb_ref, gb_ref))
    chains = []
    for n, d in [(n, d) for n in range(nb) for d in range(2)]:
        q_ref, k_ref, v_ref, g_ref = (r.at[n] for r in refs[d])
        raw = g_ref[...]
        comp = jnp.where(glane < 2 * GDN_HEADS, _sigmoid(raw), -jnp.exp(alog) * _softplus(raw + dtb))
        gexp = _dot_split(comp, esel_ref[d])
        if d == 0:
            incl, strict, upto, tri = jl <= row, jl < row, row <= jl, bd & (c2 <= r2)
        else:
            incl, strict, upto, tri = jl >= row, jl > row, row >= jl, bd & (c2 >= r2)
        last = CHUNK - 1 if d == 0 else 0
        g_all = gexp[:, w4:2 * w4]
        g_hi, g_lo = _split2(g_all)
        tri = tri.astype(BF16)
        gcum_all = _dot(tri, g_hi) + _dot(tri, g_lo)
        for c in range(NCH):
            rows = slice(c * CHUNK, (c + 1) * CHUNK)
            ch = dict(bi=n, d=d, c=c, q=q_ref[rows, :], k=k_ref[rows, :], v=v_ref[rows, :], beta=gexp[rows, 0:w4],
                      gcum=gcum_all[rows], incl=incl, strict=strict, last=last)
            grow = jnp.sum(jnp.where(upto, g_all[rows], 0.0), axis=0, keepdims=True)
            ch["decay"] = jnp.where(incl, jnp.exp(jnp.minimum(ch["gcum"] - grow, 0.0)), 0.0)
            ch["kbeta"] = ch["k"] * ch["beta"]
            chains.append(ch)

    for ch in chains:
        prod = _dot_nt(jnp.concatenate([ch["kbeta"], ch["q"]], axis=0).astype(BF16), expand(ch["k"]))
        ch["n"] = jnp.where(ch["strict"], prod[0:CHUNK] * ch["decay"], 0.0)
        ch["qk"] = jnp.where(ch["incl"], prod[CHUNK:2 * CHUNK] * ch["decay"], 0.0)
        n16 = jnp.where(blk16, ch["n"], 0.0)
        ch["t"] = eye - n16
        ch["pw"] = n16
    for ch in chains:
        ch["pw"] = mm(ch["pw"], ch["pw"])
    for _ in range(2):
        for ch in chains:
            both = mm(jnp.concatenate([ch["t"], ch["pw"]], axis=0), ch["pw"])
            ch["t"] = ch["t"] + both[0:CHUNK]
            ch["pw"] = both[CHUNK:2 * CHUNK]
    for ch in chains:
        ch["t"] = ch["t"] + mm(ch["t"], ch["pw"])
    for off in (jnp.where(blk32 & ~blk16, 1.0, 0.0), jnp.where(blk32, 0.0, 1.0)):
        for ch in chains:
            ch["te"] = mm(ch["t"], ch["n"] * off)
        for ch in chains:
            ch["t"] = ch["t"] - mm(ch["te"], ch["t"])
    for ch in chains:
        eg = jnp.exp(ch["gcum"])
        ch["u"] = mm(ch["t"], ch["v"] * ch["beta"])
        w = mm(ch["t"], ch["kbeta"] * eg)
        glast = ch["gcum"][ch["last"]:ch["last"] + 1, :]
        ch["wq"] = jnp.concatenate([w, ch["q"] * eg], axis=0).astype(BF16)
        ch["kg"] = (ch["k"] * jnp.exp(glast - ch["gcum"])).astype(BF16)
        ch["eglast"] = jnp.exp(glast)

    by_seq = {(ch["bi"], ch["d"], ch["c"]): ch for ch in chains}
    seqs = [(n, d) for n in range(nb) for d in range(2)]
    state = {sq: s_ref[2 * sq[0] + sq[1]] for sq in seqs}
    res = {}
    for i in range(NCH):
        cur = {sq: by_seq[sq[0], sq[1], i if sq[1] == 0 else NCH - 1 - i] for sq in seqs}
        ws = {sq: _dot(cur[sq]["wq"], state[sq].astype(BF16)) for sq in seqs}
        v_new = {sq: cur[sq]["u"] - ws[sq][0:CHUNK] for sq in seqs}
        for sq in seqs:
            res[sq + (cur[sq]["c"],)] = ws[sq][CHUNK:2 * CHUNK] + mm(cur[sq]["qk"], v_new[sq])
        for sq in seqs:
            upd = _dot_tn(cur[sq]["kg"], v_new[sq].astype(BF16))
            state[sq] = state[sq] * cur[sq]["eglast"] + jnp.where(bd, upd, 0.0)
    outs = (of_ref, ob_ref)
    for n, d in seqs:
        s_ref[2 * n + d] = state[n, d]
        for c in range(NCH):
            outs[d][n, c * CHUNK:(c + 1) * CHUNK, :] = res[n, d, c]


def _gdn_scan(qn, kn, vn, p, esel, gvec, layer):
    bsz, t, _ = p.shape
    nsteps = t // GB
    fwd = lambda b, i: (b, i, 0)
    bwd = lambda b, i: (b, jnp.where(i == 0, 0, nsteps - i), 0)
    gcol = C_GATE // LANE
    fwd_g = lambda b, i: (b, i, gcol)
    bwd_g = lambda b, i: (b, jnp.where(i == 0, 0, nsteps - i), gcol)
    nb = math.gcd(bsz, GDN_NB)
    blk = lambda im: pl.BlockSpec((nb, GB, GDN_W), im)
    out = jax.ShapeDtypeStruct((bsz, t, GDN_W), F32)
    return pl.pallas_call(
        _gdn_scan_kernel,
        grid=(bsz // nb, nsteps),
        in_specs=[blk(fwd), blk(fwd), blk(fwd), pl.BlockSpec((nb, GB, LANE), fwd_g),
                  blk(bwd), blk(bwd), blk(bwd), pl.BlockSpec((nb, GB, LANE), bwd_g),
                  pl.BlockSpec((2, LANE, 2 * GDN_W), lambda b, i: (0, 0, 0)),
                  pl.BlockSpec((None, 2, LANE), lambda b, i: (layer, 0, 0))],
        out_specs=[blk(fwd), blk(bwd)],
        out_shape=[out, out],
        scratch_shapes=[pltpu.VMEM((2 * nb, GDN_W, GDN_W), F32)],
        compiler_params=pltpu.CompilerParams(vmem_limit_bytes=VMEM_BIG),
        name="gdn_scan",
    )(qn, kn, vn, p, qn, kn, vn, p, esel, gvec)


def _outproj_kernel(*refs, n_tok, skip):
    (h_ref, nb_ref, of_ref, ob_ref, zg_ref, m_ref, w_ref, nw_ref, hsum_ref, lng_ref, lnb_ref, o_ref) = refs[n_tok:]
    o = of_ref[...] + ob_ref[...]
    ms = _head_sum_sq(o, hsum_ref[...]) * (1.0 / HD)
    og = o * lax.rsqrt(ms + NORM_EPS) * nw_ref[...] * _silu(zg_ref[...])
    mixed = jnp.concatenate([h_ref[...].astype(BF16), nb_ref[...].astype(BF16), og.astype(BF16)], axis=1)
    y = _dot(mixed, w_ref[...])
    gate = m_ref[...][:, 2 * D:3 * D]
    z = DEEPNORM_ALPHA * _load_tokens(refs[:n_tok], pl.program_id(1) + skip) + gate * y
    mu = jnp.mean(z, axis=-1, keepdims=True)
    zc = z - mu
    var = jnp.mean(zc * zc, axis=-1, keepdims=True)
    o_ref[...] = zc * lax.rsqrt(var + LN_EPS) * lng_ref[...] + lnb_ref[...]


def _out_proj(tokens, hg, nb, o_f, o_b, p, modsel, w_out_b, nw4, hsum4, ln_g, ln_b, layer, latent_only):
    bsz, t = _token_shape(tokens)
    skip = LC // TM if latent_only else 0
    nt = t // TM - skip
    tok_specs, tok_ops = _token_specs(tokens, skip)
    row = lambda width: pl.BlockSpec((None, TM, width), lambda b, i: (b, i + skip, 0))
    vec = lambda: pl.BlockSpec((None, 1, D), lambda b, i: (layer, 0, 0))
    return pl.pallas_call(
        functools.partial(_outproj_kernel, n_tok=len(tok_ops), skip=skip),
        grid=(bsz, nt),
        in_specs=tok_specs + [
                  row(RG_W), row(NA_W), row(GDN_W), row(GDN_W),
                  pl.BlockSpec((None, TM, GDN_W), lambda b, i: (b, i + skip, C_ZG // GDN_W)),
                  pl.BlockSpec((None, None, 1, 3 * D),
                               lambda b, i: (layer, 2 * b + jnp.minimum(i + skip, 1), 0, 0)),
                  pl.BlockSpec((None, D, D), lambda b, i: (layer, 0, 0)),
                  pl.BlockSpec((None, 1, GDN_W), lambda b, i: (layer, 0, 0)),
                  pl.BlockSpec((GDN_W, GDN_W), lambda b, i: (0, 0)),
                  vec(), vec()],
        out_specs=pl.BlockSpec((None, TM, D), lambda b, i: (b, i, 0)),
        out_shape=jax.ShapeDtypeStruct((bsz, nt * TM, D), F32),
        compiler_params=pltpu.CompilerParams(vmem_limit_bytes=VMEM_BIG),
        name="out_proj",
    )(*tok_ops, hg, nb, o_f, o_b, p, modsel, w_out_b, nw4, hsum4, ln_g, ln_b)


def _na_bias_table(na_rpb):
    jq = np.arange(GRID_W)
    col_start = np.clip(jq - NA_COLS // 2, 0, GRID_W - NA_COLS)
    kc = np.arange(GRID_W)
    inside = (kc[None, :] >= col_start[:, None]) & (kc[None, :] < col_start[:, None] + NA_COLS)
    col_off = kc[None, :] - jq[:, None] + NA_COLS - 1
    onehot = (np.arange(2 * NA_COLS - 1)[:, None, None] == col_off[None]).astype(np.float32)
    toeplitz = jnp.einsum('lhrx,xjc->lhrjc', na_rpb, jnp.asarray(onehot), precision=HIGHEST)
    tab = jnp.where(inside[None, None, None], toeplitz, NEG)
    tab = jnp.concatenate([tab, jnp.full((DEPTH, NA_HEADS, 1, GRID_W, GRID_W), NEG, F32)], axis=2)
    return jnp.concatenate([tab, tab], axis=-1)


def _rope_tables(seq):
    pos = jnp.arange(seq)
    rows_pos, cols_pos = pos // GRID_W, pos % GRID_W
    half = HD // 2
    nf = half // 2
    inv_freq = ROPE_BASE ** (-jnp.arange(nf, dtype=F32) / nf)
    lane = np.arange(LANE)
    jl = lane % HD
    use_row = jl < half
    f = (jl % half) % nf
    sign = np.where((jl % half) < nf, -1.0, 1.0).astype(np.float32)
    pos_l = jnp.where(use_row[None, :], rows_pos[:, None], cols_pos[:, None]).astype(F32)
    ang = pos_l * inv_freq[f][None, :]
    cos_t = jnp.concatenate([jnp.ones((LC, LANE), F32), jnp.cos(ang)], axis=0)
    sin_t = jnp.concatenate([jnp.zeros((LC, LANE), F32), jnp.sin(ang) * sign[None, :]], axis=0)
    return cos_t, sin_t


def _gate_select():
    e = np.zeros((2, LANE, 2 * GDN_W), np.float32)
    for d in range(2):
        for l in range(GDN_W):
            h = l // HD
            e[d, d * GDN_HEADS + h, l] = 1.0
            e[d, 2 * GDN_HEADS + d * GDN_HEADS + h, GDN_W + l] = 1.0
    return jnp.asarray(e).astype(BF16)


def _head_sum(width):
    i = np.arange(width)
    return jnp.asarray((i[:, None] // HD == i[None, :] // HD).astype(np.float32)).astype(BF16)


def kernel(x, c, ctx, c_ctx, w_mod, b_mod, w_in, conv_w, rg_wa, rg_ba, rg_wx, rg_bx, rg_lam, na_rpb, gdn_alog,
           gdn_dtb, gdn_nw, w_out, ln_g, ln_b):
    bsz, seq, _ = x.shape
    assert ctx.shape[1] == LC and seq == GRID_W * GRID_W

    c8 = jnp.concatenate([c, c_ctx[None], jnp.zeros((SUB - bsz - 1, D), F32)], axis=0)
    w_in_p = _reorder_in_cols(w_in.astype(BF16))
    w_out_b = w_out.astype(BF16)
    ns = RG_W // LANE
    eye2 = jnp.eye(2, dtype=F32)

    def block_diag(wt):
        wt = wt.reshape(DEPTH, 2, ns, 2, RG_BLOCK, RG_BLOCK)
        return jnp.einsum('ldsjae,jk->ldsjake', wt, eye2).reshape(DEPTH, 2, ns, LANE, LANE)

    wa, wx = block_diag(rg_wa), block_diag(rg_wx)
    rg_w = jnp.concatenate([wa[:, 0], wx[:, 0], wa[:, 1], wx[:, 1]], axis=-1).astype(BF16)
    slab = lambda v: v.reshape(DEPTH, ns, 1, LANE)
    rg_b = jnp.concatenate([slab(rg_ba[:, 0]), slab(rg_bx[:, 0]), slab(rg_ba[:, 1]), slab(rg_bx[:, 1])], axis=-1)
    rg_lam2 = jnp.concatenate([slab(rg_lam[:, 0]), slab(rg_lam[:, 1])], axis=-1)
    conv_w3 = jnp.stack([conv_w[:, :, C_GQ:C_GK], conv_w[:, :, C_GK:C_GV], conv_w[:, :, C_GV:C_ZA]], axis=1)
    bias_tab = _na_bias_table(na_rpb)
    cos_t, sin_t = _rope_tables(seq)
    esel = _gate_select()
    pad = jnp.zeros((DEPTH, LANE - 4 * GDN_HEADS), F32)
    zero8 = jnp.zeros((DEPTH, 2 * GDN_HEADS), F32)
    gvec = jnp.stack([jnp.concatenate([zero8, gdn_alog.reshape(DEPTH, -1), pad], axis=-1),
                      jnp.concatenate([zero8, gdn_dtb.reshape(DEPTH, -1), pad], axis=-1)], axis=1)
    nw4 = jnp.tile(gdn_nw, (1, GDN_HEADS)).reshape(DEPTH, 1, GDN_W)
    hsum2, hsum4 = _head_sum(LANE), _head_sum(GDN_W)
    ln_g3, ln_b3 = ln_g.reshape(DEPTH, 1, D), ln_b.reshape(DEPTH, 1, D)

    mods = _modulation(c8, w_mod, b_mod)
    ctx_rows = jnp.broadcast_to(mods[:, bsz:bsz + 1], (DEPTH, bsz, 3 * D))
    modsel = jnp.stack([ctx_rows, mods[:, :bsz]], axis=2).reshape(DEPTH, 2 * bsz, 1, 3 * D)

    xa = (ctx, x)
    for layer in range(DEPTH):
        p, kv16 = _in_proj(xa, modsel, w_in_p, layer)
        hg = _rglru(p, conv_w, rg_w, rg_b, rg_lam2, layer)
        nb = _natten(p, kv16, bias_tab, layer)
        qn, kn, vn = _gdn_prep(p, conv_w3, cos_t, sin_t, hsum2, layer)
        o_f, o_b = _gdn_scan(qn, kn, vn, p, esel, gvec, layer)
        xa = _out_proj(xa, hg, nb, o_f, o_b, p, modsel, w_out_b, nw4, hsum4, ln_g3, ln_b3, layer,
                       latent_only=layer == DEPTH - 1)
    return xa
```

```python
import functools
import math

import numpy as np
import jax
import jax.numpy as jnp
from jax import lax
from jax.experimental import pallas as pl
from jax.experimental.pallas import tpu as pltpu

F32 = jnp.float32
BF16 = jnp.bfloat16
HIGHEST = lax.Precision.HIGHEST

D = 1024
DEPTH = 4
LC = 256
GRID_W = 64
CONV_K = 4
RG_W = 384
RG_BLOCK = 64
RG_C = 8.0
NA_HEADS = 6
NA_W = 384
NA_ROWS = 8
NA_COLS = 16
GDN_HEADS = 4
GDN_W = 256
HD = 64
CHUNK = 64
ROPE_BASE = 10000.0
DEEPNORM_ALPHA = (2.0 * DEPTH) ** 0.25
LN_EPS = 1e-5
NORM_EPS = 1e-6
NEG = -1e30

C_RGX = 0
C_GQ = 384
C_GK = 640
C_GV = 896
C_ZA = 1152
C_NQ = 1536
C_ZN = 1920
C_ZG = 2304
C_GATE = 2560
D_F32 = 2688
C_NK = 2688
C_NV = 3072
D_INP = 3456
N_GATE = 4 * GDN_HEADS


W_KV0 = C_ZN
W_KV1 = W_KV0 + 2 * NA_W
D_IN = W_KV1 + NA_W + GDN_W + N_GATE

LANE = 128
SUB = 8
TM = 256
RC = 128
VMEM_BIG = 56 * 1024 * 1024


def _sigmoid(x):
    return 0.5 * jnp.tanh(0.5 * x) + 0.5


def _silu(x):
    return x * _sigmoid(x)


def _softplus(x):
    return jnp.maximum(x, 0.0) + jnp.log1p(jnp.exp(-jnp.abs(x)))


def _dot(a, b, precision=None):
    return jnp.dot(a, b, preferred_element_type=F32, precision=precision)


def _split2(x):
    hi = x.astype(BF16)
    return hi, (x - hi.astype(F32)).astype(BF16)


def _dot_split(x, onehot):
    hi, lo = _split2(x)
    return _dot(hi, onehot) + _dot(lo, onehot)


def _dot_nt(a, b, precision=None):
    return lax.dot_general(a, b, (((1,), (1,)), ((), ())), preferred_element_type=F32, precision=precision)


def _dot_tn(a, b, precision=None):
    return lax.dot_general(a, b, (((0,), (0,)), ((), ())), preferred_element_type=F32, precision=precision)


def _mod_kernel(c_ref, w_ref, b_ref, o_ref):
    s = _silu(c_ref[...])
    o_ref[...] = _dot(s, w_ref[...], HIGHEST) + b_ref[...]


def _modulation(c8, w_mod, b_mod):
    nb = 3 * D // D
    return pl.pallas_call(
        _mod_kernel,
        grid=(DEPTH, nb),
        in_specs=[pl.BlockSpec((SUB, D), lambda l, j: (0, 0)),
                  pl.BlockSpec((None, D, D), lambda l, j: (l, 0, j)),
                  pl.BlockSpec((None, 1, D), lambda l, j: (l, 0, j))],
        out_specs=pl.BlockSpec((None, SUB, D), lambda l, j: (l, 0, j)),
        out_shape=jax.ShapeDtypeStruct((DEPTH, SUB, 3 * D), F32),
        name="modulation",
    )(c8, w_mod, b_mod.reshape(DEPTH, 1, 3 * D))


def _token_specs(tokens, skip=0):
    assert LC == TM
    if isinstance(tokens, tuple):
        return [pl.BlockSpec((None, LC, D), lambda b, i: (b, 0, 0)),
                pl.BlockSpec((None, TM, D), lambda b, i: (b, jnp.maximum(i + skip - 1, 0), 0))], list(tokens)
    return [pl.BlockSpec((None, TM, D), lambda b, i: (b, i + skip, 0))], [tokens]


def _load_tokens(refs, tile):
    if len(refs) == 2:
        return jnp.where(tile == 0, refs[0][...], refs[1][...])
    return refs[0][...]


def _token_shape(tokens):
    if isinstance(tokens, tuple):
        return tokens[1].shape[0], tokens[0].shape[1] + tokens[1].shape[1]
    return tokens.shape[0], tokens.shape[1]


def _inproj_kernel(*refs, n_tok):
    m_ref, w_ref, o_ref, kv_ref = refs[n_tok:]
    m = m_ref[...]
    shift = m[:, :D]
    scale = m[:, D:2 * D]
    u = _load_tokens(refs[:n_tok], pl.program_id(1)) * (1.0 + scale) + shift
    p = _dot(u.astype(BF16), w_ref[...])
    o_ref[:, :W_KV0] = p[:, :W_KV0]
    o_ref[:, W_KV0:] = p[:, W_KV1:]
    kv_ref[...] = p[:, W_KV0:W_KV1].astype(BF16)


def _in_proj(tokens, modsel, w_in_p, layer):
    bsz, t = _token_shape(tokens)
    nt = t // TM
    tok_specs, tok_ops = _token_specs(tokens)
    return pl.pallas_call(
        functools.partial(_inproj_kernel, n_tok=len(tok_ops)),
        grid=(bsz, nt),
        in_specs=tok_specs + [
                  pl.BlockSpec((None, None, 1, 3 * D), lambda b, i: (layer, 2 * b + jnp.minimum(i, 1), 0, 0)),
                  pl.BlockSpec((None, D, D_INP), lambda b, i: (layer, 0, 0))],
        out_specs=[pl.BlockSpec((None, TM, D_F32), lambda b, i: (b, i, 0)),
                   pl.BlockSpec((None, TM, 2 * NA_W), lambda b, i: (b, i, 0))],
        out_shape=[jax.ShapeDtypeStruct((bsz, t, D_F32), F32), jax.ShapeDtypeStruct((bsz, t, 2 * NA_W), BF16)],
        compiler_params=pltpu.CompilerParams(vmem_limit_bytes=VMEM_BIG),
        name="in_proj",
    )(*tok_ops, modsel, w_in_p)


PAD_ROWS = 3 * SUB


def _fill_padded(src_ref, pad_ref, t):
    zeros = jnp.zeros((SUB, LANE), F32)
    pad_ref[0:SUB, :] = zeros
    pad_ref[SUB + LC:2 * SUB + LC, :] = zeros
    pad_ref[t + 2 * SUB:t + 3 * SUB, :] = zeros

    def body(c, carry):
        src = pl.multiple_of(c * LC, LC)
        dst = pl.multiple_of(src + SUB + jnp.where(c >= 1, SUB, 0), SUB)
        pad_ref[pl.ds(dst, LC), :] = src_ref[pl.ds(src, LC), :]
        return carry

    lax.fori_loop(0, t // LC, body, 0)


def _chunk_loop(n, body, unroll):
    main = n - n % unroll
    lax.fori_loop(0, main, body, 0, unroll=unroll)
    if main < n:
        lax.fori_loop(main, n, body, 0, unroll=True)


def _conv_chunk(pad_ref, cw, t0):
    base = pl.multiple_of(t0 + jnp.where(t0 >= LC, SUB, 0), SUB) + SUB
    acc = pad_ref[pl.ds(base, RC), :] * cw[2:3, :]
    for k in (0, 1, 3):
        acc = acc + pad_ref[pl.ds(base + (k - CONV_K // 2), RC), :] * cw[k:k + 1, :]
    return acc


def _scan8(a, b, row, reverse):
    for s in (1, 2, 4):
        if reverse:
            sh, ok = SUB - s, row < SUB - s
        else:
            sh, ok = s, row >= s
        a_sh = jnp.where(ok, pltpu.roll(a, sh, 0), 1.0)
        b_sh = jnp.where(ok, pltpu.roll(b, sh, 0), 0.0)
        b = a * b_sh + b
        a = a * a_sh
    return a, b


def _rglru_kernel(x_ref, z_ref, cw_ref, w_ref, bias_ref, lam_ref, o_ref,
                  pad_ref, a0_ref, b0_ref, a1_ref, b1_ref, hf_ref, hb_ref):
    t = x_ref.shape[0]
    _fill_padded(x_ref, pad_ref, t)
    cw = cw_ref[...]
    w = w_ref[...]
    bias = bias_ref[...]
    ls = -_softplus(-lam_ref[...])

    def gates(c, carry):
        t0 = pl.multiple_of(c * RC, RC)
        u = _conv_chunk(pad_ref, cw, t0)
        g = _dot(u.astype(BF16), w) + bias
        for d, (a_ref, b_ref) in enumerate(((a0_ref, b0_ref), (a1_ref, b1_ref))):
            r = _sigmoid(g[:, (2 * d) * LANE:(2 * d + 1) * LANE])
            i = _sigmoid(g[:, (2 * d + 1) * LANE:(2 * d + 2) * LANE])
            log_a = RG_C * r * ls[:, d * LANE:(d + 1) * LANE]
            a = jnp.exp(log_a)
            mult = jnp.sqrt(-jnp.tanh(log_a) * (a * a + 1.0))
            a_ref[pl.ds(t0, RC), :] = a
            b_ref[pl.ds(t0, RC), :] = mult * (i * u)
        return carry

    _chunk_loop(t // RC, gates, 8)

    row = lax.broadcasted_iota(jnp.int32, (SUB, LANE), 0)
    n_tiles = t // SUB
    n_ctx = LC // SUB

    def scan(i, carry):
        cf, cb = carry
        rf = pl.multiple_of(i * SUB, SUB)
        af, bf = _scan8(a0_ref[pl.ds(rf, SUB), :], b0_ref[pl.ds(rf, SUB), :], row, False)
        hf = bf + af * cf
        hf_ref[pl.ds(rf, SUB), :] = hf
        cf = jnp.broadcast_to(hf[SUB - 1:SUB, :], (SUB, LANE))
        j = jnp.where(i < n_ctx, n_ctx - 1 - i, n_tiles + n_ctx - 1 - i)
        rb = pl.multiple_of(j * SUB, SUB)
        ab, bb = _scan8(a1_ref[pl.ds(rb, SUB), :], b1_ref[pl.ds(rb, SUB), :], row, True)
        hb = bb + ab * cb
        hb_ref[pl.ds(rb, SUB), :] = hb
        cb = jnp.broadcast_to(hb[0:1, :], (SUB, LANE))
        return cf, cb

    zero = jnp.zeros((SUB, LANE), F32)
    lax.fori_loop(0, n_tiles, scan, (zero, zero), unroll=16)

    def finish(c, carry):
        t0 = pl.multiple_of(c * RC, RC)
        h = hf_ref[pl.ds(t0, RC), :] + hb_ref[pl.ds(t0, RC), :]
        o_ref[pl.ds(t0, RC), :] = h * _silu(z_ref[pl.ds(t0, RC), :])
        return carry

    _chunk_loop(t // RC, finish, 8)


def _rglru(p, conv_w, rg_w, rg_b, rg_lam2, layer):
    bsz, t, _ = p.shape
    ns = RG_W // LANE
    seq = lambda: pltpu.VMEM((t, LANE), F32)
    return pl.pallas_call(
        _rglru_kernel,
        grid=(bsz, ns),
        in_specs=[pl.BlockSpec((None, t, LANE), lambda b, s: (b, 0, C_RGX // LANE + s)),
                  pl.BlockSpec((None, t, LANE), lambda b, s: (b, 0, C_ZA // LANE + s)),
                  pl.BlockSpec((None, CONV_K, LANE), lambda b, s: (layer, 0, C_RGX // LANE + s)),
                  pl.BlockSpec((None, None, LANE, 4 * LANE), lambda b, s: (layer, s, 0, 0)),
                  pl.BlockSpec((None, None, 1, 4 * LANE), lambda b, s: (layer, s, 0, 0)),
                  pl.BlockSpec((None, None, 1, 2 * LANE), lambda b, s: (layer, s, 0, 0))],
        out_specs=pl.BlockSpec((None, t, LANE), lambda b, s: (b, 0, s)),
        out_shape=jax.ShapeDtypeStruct((bsz, t, RG_W), F32),
        scratch_shapes=[pltpu.VMEM((t + PAD_ROWS, LANE), F32), seq(), seq(), seq(), seq(), seq(), seq()],
        compiler_params=pltpu.CompilerParams(vmem_limit_bytes=VMEM_BIG),
        name="rglru",
    )(p, p, conv_w, rg_w, rg_b, rg_lam2)


NA_QR = 2
NA_NQ = NA_QR * GRID_W
NA_GROUPS = 2
NA_BQ = NA_GROUPS * NA_NQ
NA_KR = 10
NA_MASKED = 2 * NA_ROWS - 1


def _na_kernel(q_ref, kv_ref, z_ref, tab_ref, o_ref):
    j = pl.program_id(1)
    n_ctx_steps = LC // NA_BQ
    ns = NA_W // LANE
    lane = lax.broadcasted_iota(jnp.int32, (NA_NQ, LANE), 1)
    first = lane < HD
    first_blk = lax.broadcasted_iota(jnp.int32, (GRID_W, LANE), 1) < HD
    chains = [(g, s) for g in range(NA_GROUPS) for s in range(ns)]

    def stacked_q(g, s):
        qs = q_ref[g * NA_NQ:(g + 1) * NA_NQ, s * LANE:(s + 1) * LANE] * (HD ** -0.5)
        return jnp.concatenate([jnp.where(first, qs, 0.0), jnp.where(first, 0.0, qs)], axis=0).astype(BF16)

    def finish(g, s, sc, vv):
        m = jnp.max(sc, axis=-1, keepdims=True)
        pr = jnp.exp(sc - m)
        den = jnp.sum(pr, axis=-1, keepdims=True)
        pv = _dot(pr.astype(BF16), vv) * (1.0 / den)
        o = jnp.where(first, pv[0:NA_NQ], pv[NA_NQ:2 * NA_NQ])
        rows, cols = slice(g * NA_NQ, (g + 1) * NA_NQ), slice(s * LANE, (s + 1) * LANE)
        o_ref[rows, cols] = o * _silu(z_ref[rows, cols])

    @pl.when(j >= n_ctx_steps)
    def _():
        n_loc = NA_KR * GRID_W
        base, idx = [], []
        for g in range(NA_GROUPS):
            r = NA_QR * (NA_GROUPS * (j - n_ctx_steps) + g)
            r0 = [jnp.clip(r + i - NA_ROWS // 2, 0, GRID_W - NA_ROWS) for i in range(NA_QR)]
            start = jnp.minimum(r0[0], GRID_W - NA_KR)
            base.append(pl.multiple_of(LC + start * GRID_W, GRID_W))
            idx.append([[jnp.where((start + wu >= r0[i]) & (start + wu < r0[i] + NA_ROWS),
                                   start + wu - (r + i) + NA_ROWS - 1, NA_MASKED) for wu in range(NA_KR)]
                        for i in range(NA_QR)])
        scores = {}
        for g, s in chains:
            kk = jnp.concatenate([kv_ref[pl.ds(base[g], n_loc), s * LANE:(s + 1) * LANE],
                                  kv_ref[0:LC, s * LANE:(s + 1) * LANE]], axis=0)
            scores[g, s] = _dot_nt(stacked_q(g, s), kk)
        for g, s in chains:
            rows = []
            for hh in range(2):
                h = 2 * s + hh
                for i in range(NA_QR):
                    rows.append(jnp.concatenate(
                        [jnp.where(first_blk, tab_ref[h, idx[g][i][2 * a]], tab_ref[h, idx[g][i][2 * a + 1]])
                         for a in range(NA_KR // 2)] + [jnp.zeros((GRID_W, LC), F32)], axis=1))
            scores[g, s] = scores[g, s] + jnp.concatenate(rows, axis=0)
        for g, s in chains:
            vv = jnp.concatenate([kv_ref[pl.ds(base[g], n_loc), NA_W + s * LANE:NA_W + (s + 1) * LANE],
                                  kv_ref[0:LC, NA_W + s * LANE:NA_W + (s + 1) * LANE]], axis=0)
            finish(g, s, scores[g, s], vv)

    @pl.when(j < n_ctx_steps)
    def _():
        scores = {(g, s): _dot_nt(stacked_q(g, s), kv_ref[0:LC, s * LANE:(s + 1) * LANE]) for g, s in chains}
        for g, s in chains:
            finish(g, s, scores[g, s], kv_ref[0:LC, NA_W + s * LANE:NA_W + (s + 1) * LANE])


def _natten(p, kv16, bias_tab, layer):
    bsz, t, _ = p.shape
    nq = t // NA_BQ
    return pl.pallas_call(
        _na_kernel,
        grid=(bsz, nq),
        in_specs=[pl.BlockSpec((None, NA_BQ, NA_W), lambda b, j: (b, j, C_NQ // NA_W)),
                  pl.BlockSpec((None, t, 2 * NA_W), lambda b, j: (b, 0, 0)),
                  pl.BlockSpec((None, NA_BQ, NA_W), lambda b, j: (b, j, C_ZN // NA_W)),
                  pl.BlockSpec((None, NA_HEADS, 2 * NA_ROWS, GRID_W, LANE), lambda b, j: (layer, 0, 0, 0, 0))],
        out_specs=pl.BlockSpec((None, NA_BQ, NA_W), lambda b, j: (b, j, 0)),
        out_shape=jax.ShapeDtypeStruct((bsz, t, NA_W), F32),
        compiler_params=pltpu.CompilerParams(vmem_limit_bytes=VMEM_BIG),
        name="natten",
    )(p, kv16, p, bias_tab)


def _swap16(x):
    lane = lax.broadcasted_iota(jnp.int32, x.shape, 1)
    return jnp.where(lane % 32 < 16, pltpu.roll(x, LANE - 16, 1), pltpu.roll(x, 16, 1))


def _head_sum_sq(x, hsum):
    return _dot_split(x * x, hsum)


def _gdn_prep_kernel(q_ref, k_ref, v_ref, cw_ref, cos_ref, sin_ref, hsum_ref, qo_ref, ko_ref, vo_ref,
                     qpad_ref, kpad_ref, vpad_ref):
    t = q_ref.shape[0]
    hsum = hsum_ref[...]
    streams = ((q_ref, qpad_ref, qo_ref, "q"), (k_ref, kpad_ref, ko_ref, "k"), (v_ref, vpad_ref, vo_ref, "v"))
    for src_ref, pad_ref, _, _ in streams:
        _fill_padded(src_ref, pad_ref, t)

    def body(c, carry):
        t0 = pl.multiple_of(c * RC, RC)
        xs = [_silu(_conv_chunk(pad_ref, cw_ref[n], t0)) for n, (_, pad_ref, _, _) in enumerate(streams)]
        ssq = [_head_sum_sq(x, hsum) for x in xs[:2]]
        for n, (_, _, dst_ref, mode) in enumerate(streams):
            x = xs[n]
            if mode != "v":
                x = x * lax.rsqrt(ssq[n] + NORM_EPS)
                x = x * cos_ref[pl.ds(t0, RC), :] + _swap16(x) * sin_ref[pl.ds(t0, RC), :]
            if mode == "q":
                x = x * (HD ** -0.5)
            dst_ref[pl.ds(t0, RC), :] = x
        return carry

    _chunk_loop(t // RC, body, 8)


def _gdn_prep(p, conv_w3, cos_t, sin_t, hsum, layer):
    bsz, t, _ = p.shape
    ns = GDN_W // LANE
    col = lambda c0: (lambda b, s: (b, 0, c0 // LANE + s))
    out = jax.ShapeDtypeStruct((bsz, t, GDN_W), F32)
    return pl.pallas_call(
        _gdn_prep_kernel,
        grid=(bsz, ns),
        in_specs=[pl.BlockSpec((None, t, LANE), col(C_GQ)),
                  pl.BlockSpec((None, t, LANE), col(C_GK)),
                  pl.BlockSpec((None, t, LANE), col(C_GV)),
                  pl.BlockSpec((None, 3, CONV_K, LANE), lambda b, s: (layer, 0, 0, s)),
                  pl.BlockSpec((t, LANE), lambda b, s: (0, 0)),
                  pl.BlockSpec((t, LANE), lambda b, s: (0, 0)),
                  pl.BlockSpec((LANE, LANE), lambda b, s: (0, 0))],
        out_specs=[pl.BlockSpec((None, t, LANE), lambda b, s: (b, 0, s))] * 3,
        out_shape=[out, out, out],
        scratch_shapes=[pltpu.VMEM((t + PAD_ROWS, LANE), F32)] * 3,
        compiler_params=pltpu.CompilerParams(vmem_limit_bytes=VMEM_BIG),
        name="gdn_prep",
    )(p, p, p, conv_w3, cos_t, sin_t, hsum)


GB = 256
NCH = GB // CHUNK
GDN_NB = 4


def _gdn_scan_kernel(qf_ref, kf_ref, vf_ref, gf_ref, qb_ref, kb_ref, vb_ref, gb_ref,
                     esel_ref, gvec_ref, of_ref, ob_ref, s_ref):
    step = pl.program_id(1)
    w4 = GDN_W
    nb = qf_ref.shape[0]

    @pl.when(step == 0)
    def _():
        s_ref[...] = jnp.zeros_like(s_ref)

    row = lax.broadcasted_iota(jnp.int32, (CHUNK, w4), 0)
    jl = lax.broadcasted_iota(jnp.int32, (CHUNK, w4), 1) % HD
    r2 = lax.broadcasted_iota(jnp.int32, (w4, w4), 0)
    c2 = lax.broadcasted_iota(jnp.int32, (w4, w4), 1)
    bd = (r2 // HD) == (c2 // HD)
    eye = (jl == row).astype(F32)
    blk16 = (row // 16) == (jl // 16)
    blk32 = (row // 32) == (jl // 32)
    alog = gvec_ref[0:1, :]
    dtb = gvec_ref[1:2, :]
    glane = lax.broadcasted_iota(jnp.int32, (GB, LANE), 1)

    def expand(y):
        yb = y.astype(BF16)
        return jnp.where(bd, jnp.concatenate([yb, yb, yb, yb], axis=0), jnp.zeros((), BF16))

    def mm(x, y):
        return _dot(x.astype(BF16), expand(y))

    refs = ((qf_ref, kf_ref, vf_ref, gf_ref), (qb_ref, kb_ref, vb_ref, gb_ref))
    chains = []
    for n, d in [(n, d) for n in range(nb) for d in range(2)]:
        q_ref, k_ref, v_ref, g_ref = (r.at[n] for r in refs[d])
        raw = g_ref[...]
        comp = jnp.where(glane < 2 * GDN_HEADS, _sigmoid(raw), -jnp.exp(alog) * _softplus(raw + dtb))
        gexp = _dot_split(comp, esel_ref[d])
        if d == 0:
            incl, strict, upto, tri = jl <= row, jl < row, row <= jl, bd & (c2 <= r2)
        else:
            incl, strict, upto, tri = jl >= row, jl > row, row >= jl, bd & (c2 >= r2)
        last = CHUNK - 1 if d == 0 else 0
        g_all = gexp[:, w4:2 * w4]
        g_hi, g_lo = _split2(g_all)
        tri = tri.astype(BF16)
        gcum_all = _dot(tri, g_hi) + _dot(tri, g_lo)
        for c in range(NCH):
            rows = slice(c * CHUNK, (c + 1) * CHUNK)
            ch = dict(bi=n, d=d, c=c, q=q_ref[rows, :], k=k_ref[rows, :], v=v_ref[rows, :], beta=gexp[rows, 0:w4],
                      gcum=gcum_all[rows], incl=incl, strict=strict, last=last)
            grow = jnp.sum(jnp.where(upto, g_all[rows], 0.0), axis=0, keepdims=True)
            ch["decay"] = jnp.where(incl, jnp.exp(jnp.minimum(ch["gcum"] - grow, 0.0)), 0.0)
            ch["kbeta"] = ch["k"] * ch["beta"]
            chains.append(ch)

    for ch in chains:
        prod = _dot_nt(jnp.concatenate([ch["kbeta"], ch["q"]], axis=0).astype(BF16), expand(ch["k"]))
        ch["n"] = jnp.where(ch["strict"], prod[0:CHUNK] * ch["decay"], 0.0)
        ch["qk"] = jnp.where(ch["incl"], prod[CHUNK:2 * CHUNK] * ch["decay"], 0.0)
        n16 = jnp.where(blk16, ch["n"], 0.0)
        ch["t"] = eye - n16
        ch["pw"] = n16
    for ch in chains:
        ch["pw"] = mm(ch["pw"], ch["pw"])
    for _ in range(2):
        for ch in chains:
            both = mm(jnp.concatenate([ch["t"], ch["pw"]], axis=0), ch["pw"])
            ch["t"] = ch["t"] + both[0:CHUNK]
            ch["pw"] = both[CHUNK:2 * CHUNK]
    for ch in chains:
        ch["t"] = ch["t"] + mm(ch["t"], ch["pw"])
    for off in (jnp.where(blk32 & ~blk16, 1.0, 0.0), jnp.where(blk32, 0.0, 1.0)):
        for ch in chains:
            ch["te"] = mm(ch["t"], ch["n"] * off)
        for ch in chains:
            ch["t"] = ch["t"] - mm(ch["te"], ch["t"])
    for ch in chains:
        eg = jnp.exp(ch["gcum"])
        ch["u"] = mm(ch["t"], ch["v"] * ch["beta"])
        w = mm(ch["t"], ch["kbeta"] * eg)
        glast = ch["gcum"][ch["last"]:ch["last"] + 1, :]
        ch["wq"] = jnp.concatenate([w, ch["q"] * eg], axis=0).astype(BF16)
        ch["kg"] = (ch["k"] * jnp.exp(glast - ch["gcum"])).astype(BF16)
        ch["eglast"] = jnp.exp(glast)

    by_seq = {(ch["bi"], ch["d"], ch["c"]): ch for ch in chains}
    seqs = [(n, d) for n in range(nb) for d in range(2)]
    state = {sq: s_ref[2 * sq[0] + sq[1]] for sq in seqs}
    res = {}
    for i in range(NCH):
        cur = {sq: by_seq[sq[0], sq[1], i if sq[1] == 0 else NCH - 1 - i] for sq in seqs}
        ws = {sq: _dot(cur[sq]["wq"], state[sq].astype(BF16)) for sq in seqs}
        v_new = {sq: cur[sq]["u"] - ws[sq][0:CHUNK] for sq in seqs}
        for sq in seqs:
            res[sq + (cur[sq]["c"],)] = ws[sq][CHUNK:2 * CHUNK] + mm(cur[sq]["qk"], v_new[sq])
        for sq in seqs:
            upd = _dot_tn(cur[sq]["kg"], v_new[sq].astype(BF16))
            state[sq] = state[sq] * cur[sq]["eglast"] + jnp.where(bd, upd, 0.0)
    outs = (of_ref, ob_ref)
    for n, d in seqs:
        s_ref[2 * n + d] = state[n, d]
        for c in range(NCH):
            outs[d][n, c * CHUNK:(c + 1) * CHUNK, :] = res[n, d, c]


def _gdn_scan(qn, kn, vn, p, esel, gvec, layer):
    bsz, t, _ = p.shape
    nsteps = t // GB
    fwd = lambda b, i: (b, i, 0)
    bwd = lambda b, i: (b, jnp.where(i == 0, 0, nsteps - i), 0)
    gcol = C_GATE // LANE
    fwd_g = lambda b, i: (b, i, gcol)
    bwd_g = lambda b, i: (b, jnp.where(i == 0, 0, nsteps - i), gcol)
    nb = math.gcd(bsz, GDN_NB)
    blk = lambda im: pl.BlockSpec((nb, GB, GDN_W), im)
    out = jax.ShapeDtypeStruct((bsz, t, GDN_W), F32)
    return pl.pallas_call(
        _gdn_scan_kernel,
        grid=(bsz // nb, nsteps),
        in_specs=[blk(fwd), blk(fwd), blk(fwd), pl.BlockSpec((nb, GB, LANE), fwd_g),
                  blk(bwd), blk(bwd), blk(bwd), pl.BlockSpec((nb, GB, LANE), bwd_g),
                  pl.BlockSpec((2, LANE, 2 * GDN_W), lambda b, i: (0, 0, 0)),
                  pl.BlockSpec((None, 2, LANE), lambda b, i: (layer, 0, 0))],
        out_specs=[blk(fwd), blk(bwd)],
        out_shape=[out, out],
        scratch_shapes=[pltpu.VMEM((2 * nb, GDN_W, GDN_W), F32)],
        compiler_params=pltpu.CompilerParams(vmem_limit_bytes=VMEM_BIG),
        name="gdn_scan",
    )(qn, kn, vn, p, qn, kn, vn, p, esel, gvec)


def _outproj_kernel(*refs, n_tok, skip):
    (h_ref, nb_ref, of_ref, ob_ref, zg_ref, m_ref, w_ref, nw_ref, hsum_ref, lng_ref, lnb_ref, o_ref) = refs[n_tok:]
    o = of_ref[...] + ob_ref[...]
    ms = _head_sum_sq(o, hsum_ref[...]) * (1.0 / HD)
    og = o * lax.rsqrt(ms + NORM_EPS) * nw_ref[...] * _silu(zg_ref[...])
    mixed = jnp.concatenate([h_ref[...].astype(BF16), nb_ref[...].astype(BF16), og.astype(BF16)], axis=1)
    y = _dot(mixed, w_ref[...])
    gate = m_ref[...][:, 2 * D:3 * D]
    z = DEEPNORM_ALPHA * _load_tokens(refs[:n_tok], pl.program_id(1) + skip) + gate * y
    mu = jnp.mean(z, axis=-1, keepdims=True)
    zc = z - mu
    var = jnp.mean(zc * zc, axis=-1, keepdims=True)
    o_ref[...] = zc * lax.rsqrt(var + LN_EPS) * lng_ref[...] + lnb_ref[...]


def _out_proj(tokens, hg, nb, o_f, o_b, p, modsel, w_out_b, nw4, hsum4, ln_g, ln_b, layer, latent_only):
    bsz, t = _token_shape(tokens)
    skip = LC // TM if latent_only else 0
    nt = t // TM - skip
    tok_specs, tok_ops = _token_specs(tokens, skip)
    row = lambda width: pl.BlockSpec((None, TM, width), lambda b, i: (b, i + skip, 0))
    vec = lambda: pl.BlockSpec((None, 1, D), lambda b, i: (layer, 0, 0))
    return pl.pallas_call(
        functools.partial(_outproj_kernel, n_tok=len(tok_ops), skip=skip),
        grid=(bsz, nt),
        in_specs=tok_specs + [
                  row(RG_W), row(NA_W), row(GDN_W), row(GDN_W),
                  pl.BlockSpec((None, TM, GDN_W), lambda b, i: (b, i + skip, C_ZG // GDN_W)),
                  pl.BlockSpec((None, None, 1, 3 * D),
                               lambda b, i: (layer, 2 * b + jnp.minimum(i + skip, 1), 0, 0)),
                  pl.BlockSpec((None, D, D), lambda b, i: (layer, 0, 0)),
                  pl.BlockSpec((None, 1, GDN_W), lambda b, i: (layer, 0, 0)),
                  pl.BlockSpec((GDN_W, GDN_W), lambda b, i: (0, 0)),
                  vec(), vec()],
        out_specs=pl.BlockSpec((None, TM, D), lambda b, i: (b, i, 0)),
        out_shape=jax.ShapeDtypeStruct((bsz, nt * TM, D), F32),
        compiler_params=pltpu.CompilerParams(vmem_limit_bytes=VMEM_BIG),
        name="out_proj",
    )(*tok_ops, hg, nb, o_f, o_b, p, modsel, w_out_b, nw4, hsum4, ln_g, ln_b)


def _na_bias_table(na_rpb):
    jq = np.arange(GRID_W)
    col_start = np.clip(jq - NA_COLS // 2, 0, GRID_W - NA_COLS)
    kc = np.arange(GRID_W)
    inside = (kc[None, :] >= col_start[:, None]) & (kc[None, :] < col_start[:, None] + NA_COLS)
    col_off = kc[None, :] - jq[:, None] + NA_COLS - 1
    onehot = (np.arange(2 * NA_COLS - 1)[:, None, None] == col_off[None]).astype(np.float32)
    toeplitz = jnp.einsum('lhrx,xjc->lhrjc', na_rpb, jnp.asarray(onehot), precision=HIGHEST)
    tab = jnp.where(inside[None, None, None], toeplitz, NEG)
    tab = jnp.concatenate([tab, jnp.full((DEPTH, NA_HEADS, 1, GRID_W, GRID_W), NEG, F32)], axis=2)
    return jnp.concatenate([tab, tab], axis=-1)


def _rope_tables(seq):
    pos = np.arange(seq)
    rows_pos, cols_pos = pos // GRID_W, pos % GRID_W
    half = HD // 2
    nf = half // 2
    inv_freq = (np.float32(ROPE_BASE) ** (-np.arange(nf, dtype=np.float32) / np.float32(nf))).astype(np.float32)
    lane = np.arange(LANE)
    jl = lane % HD
    use_row = jl < half
    f = (jl % half) % nf
    sign = np.where((jl % half) < nf, -1.0, 1.0)
    pos_l = np.where(use_row[None, :], rows_pos[:, None], cols_pos[:, None]).astype(np.float32)
    ang = (pos_l * inv_freq[f][None, :]).astype(np.float64)
    cos_t = np.concatenate([np.ones((LC, LANE)), np.cos(ang)], axis=0).astype(np.float32)
    sin_t = np.concatenate([np.zeros((LC, LANE)), np.sin(ang) * sign[None, :]], axis=0).astype(np.float32)
    return jnp.asarray(cos_t), jnp.asarray(sin_t)


def _gate_select():
    e = np.zeros((2, LANE, 2 * GDN_W), np.float32)
    for d in range(2):
        for l in range(GDN_W):
            h = l // HD
            e[d, d * GDN_HEADS + h, l] = 1.0
            e[d, 2 * GDN_HEADS + d * GDN_HEADS + h, GDN_W + l] = 1.0
    return jnp.asarray(e).astype(BF16)


def _head_sum(width):
    i = np.arange(width)
    return jnp.asarray((i[:, None] // HD == i[None, :] // HD).astype(np.float32)).astype(BF16)


def kernel(x, c, ctx, c_ctx, w_mod, b_mod, w_in, conv_w, rg_wa, rg_ba, rg_wx, rg_bx, rg_lam, na_rpb, gdn_alog,
           gdn_dtb, gdn_nw, w_out, ln_g, ln_b):
    bsz, seq, _ = x.shape
    assert ctx.shape[1] == LC and seq == GRID_W * GRID_W

    c8 = jnp.concatenate([c, c_ctx[None], jnp.zeros((SUB - bsz - 1, D), F32)], axis=0)
    w_in_p = jnp.pad(w_in.astype(BF16), ((0, 0), (0, 0), (0, D_INP - D_IN)))
    w_out_b = w_out.astype(BF16)
    ns = RG_W // LANE
    eye2 = jnp.eye(2, dtype=F32)

    def block_diag(wt):
        wt = wt.reshape(DEPTH, 2, ns, 2, RG_BLOCK, RG_BLOCK)
        return jnp.einsum('ldsjae,jk->ldsjake', wt, eye2).reshape(DEPTH, 2, ns, LANE, LANE)

    wa, wx = block_diag(rg_wa), block_diag(rg_wx)
    rg_w = jnp.concatenate([wa[:, 0], wx[:, 0], wa[:, 1], wx[:, 1]], axis=-1).astype(BF16)
    slab = lambda v: v.reshape(DEPTH, ns, 1, LANE)
    rg_b = jnp.concatenate([slab(rg_ba[:, 0]), slab(rg_bx[:, 0]), slab(rg_ba[:, 1]), slab(rg_bx[:, 1])], axis=-1)
    rg_lam2 = jnp.concatenate([slab(rg_lam[:, 0]), slab(rg_lam[:, 1])], axis=-1)
    conv_w3 = jnp.stack([conv_w[:, :, C_GQ:C_GK], conv_w[:, :, C_GK:C_GV], conv_w[:, :, C_GV:C_ZA]], axis=1)
    bias_tab = _na_bias_table(na_rpb)
    cos_t, sin_t = _rope_tables(seq)
    esel = _gate_select()
    pad = jnp.zeros((DEPTH, LANE - 4 * GDN_HEADS), F32)
    zero8 = jnp.zeros((DEPTH, 2 * GDN_HEADS), F32)
    gvec = jnp.stack([jnp.concatenate([zero8, gdn_alog.reshape(DEPTH, -1), pad], axis=-1),
                      jnp.concatenate([zero8, gdn_dtb.reshape(DEPTH, -1), pad], axis=-1)], axis=1)
    nw4 = jnp.tile(gdn_nw, (1, GDN_HEADS)).reshape(DEPTH, 1, GDN_W)
    hsum2, hsum4 = _head_sum(LANE), _head_sum(GDN_W)
    ln_g3, ln_b3 = ln_g.reshape(DEPTH, 1, D), ln_b.reshape(DEPTH, 1, D)

    mods = _modulation(c8, w_mod, b_mod)
    ctx_rows = jnp.broadcast_to(mods[:, bsz:bsz + 1], (DEPTH, bsz, 3 * D))
    modsel = jnp.stack([ctx_rows, mods[:, :bsz]], axis=2).reshape(DEPTH, 2 * bsz, 1, 3 * D)

    xa = (ctx, x)
    for layer in range(DEPTH):
        p, kv16 = _in_proj(xa, modsel, w_in_p, layer)
        hg = _rglru(p, conv_w, rg_w, rg_b, rg_lam2, layer)
        nb = _natten(p, kv16, bias_tab, layer)
        qn, kn, vn = _gdn_prep(p, conv_w3, cos_t, sin_t, hsum2, layer)
        o_f, o_b = _gdn_scan(qn, kn, vn, p, esel, gvec, layer)
        xa = _out_proj(xa, hg, nb, o_f, o_b, p, modsel, w_out_b, nw4, hsum4, ln_g3, ln_b3, layer,
                       latent_only=layer == DEPTH - 1)
    return xa
```

```python
import functools
import math

import numpy as np
import jax
import jax.numpy as jnp
from jax import lax
from jax.experimental import pallas as pl
from jax.experimental.pallas import tpu as pltpu

F32 = jnp.float32
BF16 = jnp.bfloat16
HIGHEST = lax.Precision.HIGHEST

D = 1024
DEPTH = 4
LC = 256
GRID_W = 64
CONV_K = 4
RG_W = 384
RG_BLOCK = 64
RG_C = 8.0
NA_HEADS = 6
NA_W = 384
NA_ROWS = 8
NA_COLS = 16
GDN_HEADS = 4
GDN_W = 256
HD = 64
CHUNK = 64
ROPE_BASE = 10000.0
DEEPNORM_ALPHA = (2.0 * DEPTH) ** 0.25
LN_EPS = 1e-5
NORM_EPS = 1e-6
NEG = -1e30

C_RGX = 0
C_GQ = 384
C_GK = 640
C_GV = 896
C_ZA = 1152
C_NQ = 1536
C_ZN = 1920
C_ZG = 2304
C_GATE = 2560
D_F32 = 2688
C_NK = 2688
C_NV = 3072
D_INP = 3456
N_GATE = 4 * GDN_HEADS


W_KV0 = C_ZN
W_KV1 = W_KV0 + 2 * NA_W
D_IN = W_KV1 + NA_W + GDN_W + N_GATE

LANE = 128
SUB = 8
TM = 256
RC = 128
VMEM_BIG = 56 * 1024 * 1024


def _sigmoid(x):
    return 0.5 * jnp.tanh(0.5 * x) + 0.5


def _silu(x):
    return x * _sigmoid(x)


def _softplus(x):
    return jnp.maximum(x, 0.0) + jnp.log1p(jnp.exp(-jnp.abs(x)))


def _dot(a, b, precision=None):
    return jnp.dot(a, b, preferred_element_type=F32, precision=precision)


def _split2(x):
    hi = x.astype(BF16)
    return hi, (x - hi.astype(F32)).astype(BF16)


def _dot_split(x, onehot):
    hi, lo = _split2(x)
    return _dot(hi, onehot) + _dot(lo, onehot)


def _dot_nt(a, b, precision=None):
    return lax.dot_general(a, b, (((1,), (1,)), ((), ())), preferred_element_type=F32, precision=precision)


def _dot_tn(a, b, precision=None):
    return lax.dot_general(a, b, (((0,), (0,)), ((), ())), preferred_element_type=F32, precision=precision)


def _mod_kernel(c_ref, w_ref, b_ref, o_ref):
    s = _silu(c_ref[...])
    o_ref[...] = _dot(s, w_ref[...], HIGHEST) + b_ref[...]


def _modulation(c8, w_mod, b_mod):
    nb = 3 * D // D
    return pl.pallas_call(
        _mod_kernel,
        grid=(DEPTH, nb),
        in_specs=[pl.BlockSpec((SUB, D), lambda l, j: (0, 0)),
                  pl.BlockSpec((None, D, D), lambda l, j: (l, 0, j)),
                  pl.BlockSpec((None, 1, D), lambda l, j: (l, 0, j))],
        out_specs=pl.BlockSpec((None, SUB, D), lambda l, j: (l, 0, j)),
        out_shape=jax.ShapeDtypeStruct((DEPTH, SUB, 3 * D), F32),
        name="modulation",
    )(c8, w_mod, b_mod.reshape(DEPTH, 1, 3 * D))


def _token_specs(tokens, skip=0):
    assert LC == TM
    if isinstance(tokens, tuple):
        return [pl.BlockSpec((None, LC, D), lambda b, i: (b, 0, 0)),
                pl.BlockSpec((None, TM, D), lambda b, i: (b, jnp.maximum(i + skip - 1, 0), 0))], list(tokens)
    return [pl.BlockSpec((None, TM, D), lambda b, i: (b, i + skip, 0))], [tokens]


def _load_tokens(refs, tile):
    if len(refs) == 2:
        return jnp.where(tile == 0, refs[0][...], refs[1][...])
    return refs[0][...]


def _token_shape(tokens):
    if isinstance(tokens, tuple):
        return tokens[1].shape[0], tokens[0].shape[1] + tokens[1].shape[1]
    return tokens.shape[0], tokens.shape[1]


def _inproj_kernel(*refs, n_tok):
    m_ref, w_ref, o_ref, kv_ref = refs[n_tok:]
    m = m_ref[...]
    shift = m[:, :D]
    scale = m[:, D:2 * D]
    u = _load_tokens(refs[:n_tok], pl.program_id(1)) * (1.0 + scale) + shift
    p = _dot(u.astype(BF16), w_ref[...])
    o_ref[:, :W_KV0] = p[:, :W_KV0]
    o_ref[:, W_KV0:] = p[:, W_KV1:]
    kv_ref[...] = p[:, W_KV0:W_KV1].astype(BF16)


def _in_proj(tokens, modsel, w_in_p, layer):
    bsz, t = _token_shape(tokens)
    nt = t // TM
    tok_specs, tok_ops = _token_specs(tokens)
    return pl.pallas_call(
        functools.partial(_inproj_kernel, n_tok=len(tok_ops)),
        grid=(bsz, nt),
        in_specs=tok_specs + [
                  pl.BlockSpec((None, None, 1, 3 * D), lambda b, i: (layer, 2 * b + jnp.minimum(i, 1), 0, 0)),
                  pl.BlockSpec((None, D, D_INP), lambda b, i: (layer, 0, 0))],
        out_specs=[pl.BlockSpec((None, TM, D_F32), lambda b, i: (b, i, 0)),
                   pl.BlockSpec((None, TM, 2 * NA_W), lambda b, i: (b, i, 0))],
        out_shape=[jax.ShapeDtypeStruct((bsz, t, D_F32), F32), jax.ShapeDtypeStruct((bsz, t, 2 * NA_W), BF16)],
        compiler_params=pltpu.CompilerParams(vmem_limit_bytes=VMEM_BIG),
        name="in_proj",
    )(*tok_ops, modsel, w_in_p)


PAD_ROWS = 3 * SUB


def _fill_padded(src_ref, pad_ref, t):
    zeros = jnp.zeros((SUB, LANE), F32)
    pad_ref[0:SUB, :] = zeros
    pad_ref[SUB + LC:2 * SUB + LC, :] = zeros
    pad_ref[t + 2 * SUB:t + 3 * SUB, :] = zeros

    def body(c, carry):
        src = pl.multiple_of(c * LC, LC)
        dst = pl.multiple_of(src + SUB + jnp.where(c >= 1, SUB, 0), SUB)
        pad_ref[pl.ds(dst, LC), :] = src_ref[pl.ds(src, LC), :]
        return carry

    lax.fori_loop(0, t // LC, body, 0)


def _chunk_loop(n, body, unroll):
    main = n - n % unroll
    lax.fori_loop(0, main, body, 0, unroll=unroll)
    if main < n:
        lax.fori_loop(main, n, body, 0, unroll=True)


def _conv_chunk(pad_ref, cw, t0):
    base = pl.multiple_of(t0 + jnp.where(t0 >= LC, SUB, 0), SUB) + SUB
    acc = pad_ref[pl.ds(base, RC), :] * cw[2:3, :]
    for k in (0, 1, 3):
        acc = acc + pad_ref[pl.ds(base + (k - CONV_K // 2), RC), :] * cw[k:k + 1, :]
    return acc


def _scan8(a, b, row, reverse):
    for s in (1, 2, 4):
        if reverse:
            sh, ok = SUB - s, row < SUB - s
        else:
            sh, ok = s, row >= s
        a_sh = jnp.where(ok, pltpu.roll(a, sh, 0), 1.0)
        b_sh = jnp.where(ok, pltpu.roll(b, sh, 0), 0.0)
        b = a * b_sh + b
        a = a * a_sh
    return a, b


def _rglru_kernel(x_ref, z_ref, cw_ref, w_ref, bias_ref, lam_ref, o_ref,
                  pad_ref, a0_ref, b0_ref, a1_ref, b1_ref, hf_ref, hb_ref):
    t = x_ref.shape[0]
    _fill_padded(x_ref, pad_ref, t)
    cw = cw_ref[...]
    w = w_ref[...]
    bias = bias_ref[...]
    ls = -_softplus(-lam_ref[...])

    def gates(c, carry):
        t0 = pl.multiple_of(c * RC, RC)
        u = _conv_chunk(pad_ref, cw, t0)
        g = _dot(u.astype(BF16), w) + bias
        for d, (a_ref, b_ref) in enumerate(((a0_ref, b0_ref), (a1_ref, b1_ref))):
            r = _sigmoid(g[:, (2 * d) * LANE:(2 * d + 1) * LANE])
            i = _sigmoid(g[:, (2 * d + 1) * LANE:(2 * d + 2) * LANE])
            log_a = RG_C * r * ls[:, d * LANE:(d + 1) * LANE]
            a = jnp.exp(log_a)
            mult = jnp.sqrt(-jnp.tanh(log_a) * (a * a + 1.0))
            a_ref[pl.ds(t0, RC), :] = a
            b_ref[pl.ds(t0, RC), :] = mult * (i * u)
        return carry

    _chunk_loop(t // RC, gates, 8)

    row = lax.broadcasted_iota(jnp.int32, (SUB, LANE), 0)
    n_tiles = t // SUB
    n_ctx = LC // SUB

    def scan(i, carry):
        cf, cb = carry
        rf = pl.multiple_of(i * SUB, SUB)
        af, bf = _scan8(a0_ref[pl.ds(rf, SUB), :], b0_ref[pl.ds(rf, SUB), :], row, False)
        hf = bf + af * cf
        hf_ref[pl.ds(rf, SUB), :] = hf
        cf = jnp.broadcast_to(hf[SUB - 1:SUB, :], (SUB, LANE))
        j = jnp.where(i < n_ctx, n_ctx - 1 - i, n_tiles + n_ctx - 1 - i)
        rb = pl.multiple_of(j * SUB, SUB)
        ab, bb = _scan8(a1_ref[pl.ds(rb, SUB), :], b1_ref[pl.ds(rb, SUB), :], row, True)
        hb = bb + ab * cb
        hb_ref[pl.ds(rb, SUB), :] = hb
        cb = jnp.broadcast_to(hb[0:1, :], (SUB, LANE))
        return cf, cb

    zero = jnp.zeros((SUB, LANE), F32)
    lax.fori_loop(0, n_tiles, scan, (zero, zero), unroll=16)

    def finish(c, carry):
        t0 = pl.multiple_of(c * RC, RC)
        h = hf_ref[pl.ds(t0, RC), :] + hb_ref[pl.ds(t0, RC), :]
        o_ref[pl.ds(t0, RC), :] = h * _silu(z_ref[pl.ds(t0, RC), :])
        return carry

    _chunk_loop(t // RC, finish, 8)


def _rglru(p, conv_w, rg_w, rg_b, rg_lam2, layer):
    bsz, t, _ = p.shape
    ns = RG_W // LANE
    seq = lambda: pltpu.VMEM((t, LANE), F32)
    return pl.pallas_call(
        _rglru_kernel,
        grid=(bsz, ns),
        in_specs=[pl.BlockSpec((None, t, LANE), lambda b, s: (b, 0, C_RGX // LANE + s)),
                  pl.BlockSpec((None, t, LANE), lambda b, s: (b, 0, C_ZA // LANE + s)),
                  pl.BlockSpec((None, CONV_K, LANE), lambda b, s: (layer, 0, C_RGX // LANE + s)),
                  pl.BlockSpec((None, None, LANE, 4 * LANE), lambda b, s: (layer, s, 0, 0)),
                  pl.BlockSpec((None, None, 1, 4 * LANE), lambda b, s: (layer, s, 0, 0)),
                  pl.BlockSpec((None, None, 1, 2 * LANE), lambda b, s: (layer, s, 0, 0))],
        out_specs=pl.BlockSpec((None, t, LANE), lambda b, s: (b, 0, s)),
        out_shape=jax.ShapeDtypeStruct((bsz, t, RG_W), F32),
        scratch_shapes=[pltpu.VMEM((t + PAD_ROWS, LANE), F32), seq(), seq(), seq(), seq(), seq(), seq()],
        compiler_params=pltpu.CompilerParams(vmem_limit_bytes=VMEM_BIG),
        name="rglru",
    )(p, p, conv_w, rg_w, rg_b, rg_lam2)


NA_QR = 2
NA_NQ = NA_QR * GRID_W
NA_GROUPS = 2
NA_BQ = NA_GROUPS * NA_NQ
NA_KR = 10
NA_MASKED = 2 * NA_ROWS - 1


def _na_kernel(q_ref, kv_ref, z_ref, tab_ref, o_ref):
    j = pl.program_id(1)
    n_ctx_steps = LC // NA_BQ
    ns = NA_W // LANE
    lane = lax.broadcasted_iota(jnp.int32, (NA_NQ, LANE), 1)
    first = lane < HD
    first_blk = lax.broadcasted_iota(jnp.int32, (GRID_W, LANE), 1) < HD
    chains = [(g, s) for g in range(NA_GROUPS) for s in range(ns)]

    def stacked_q(g, s):
        qs = q_ref[g * NA_NQ:(g + 1) * NA_NQ, s * LANE:(s + 1) * LANE] * (HD ** -0.5)
        return jnp.concatenate([jnp.where(first, qs, 0.0), jnp.where(first, 0.0, qs)], axis=0).astype(BF16)

    def finish(g, s, sc, vv):
        m = jnp.max(sc, axis=-1, keepdims=True)
        pr = jnp.exp(sc - m)
        den = jnp.sum(pr, axis=-1, keepdims=True)
        pv = _dot(pr.astype(BF16), vv) * (1.0 / den)
        o = jnp.where(first, pv[0:NA_NQ], pv[NA_NQ:2 * NA_NQ])
        rows, cols = slice(g * NA_NQ, (g + 1) * NA_NQ), slice(s * LANE, (s + 1) * LANE)
        o_ref[rows, cols] = o * _silu(z_ref[rows, cols])

    @pl.when(j >= n_ctx_steps)
    def _():
        n_loc = NA_KR * GRID_W
        base, idx = [], []
        for g in range(NA_GROUPS):
            r = NA_QR * (NA_GROUPS * (j - n_ctx_steps) + g)
            r0 = [jnp.clip(r + i - NA_ROWS // 2, 0, GRID_W - NA_ROWS) for i in range(NA_QR)]
            start = jnp.minimum(r0[0], GRID_W - NA_KR)
            base.append(pl.multiple_of(LC + start * GRID_W, GRID_W))
            idx.append([[jnp.where((start + wu >= r0[i]) & (start + wu < r0[i] + NA_ROWS),
                                   start + wu - (r + i) + NA_ROWS - 1, NA_MASKED) for wu in range(NA_KR)]
                        for i in range(NA_QR)])
        scores = {}
        for g, s in chains:
            kk = jnp.concatenate([kv_ref[pl.ds(base[g], n_loc), s * LANE:(s + 1) * LANE],
                                  kv_ref[0:LC, s * LANE:(s + 1) * LANE]], axis=0)
            scores[g, s] = _dot_nt(stacked_q(g, s), kk)
        for g, s in chains:
            rows = []
            for hh in range(2):
                h = 2 * s + hh
                for i in range(NA_QR):
                    rows.append(jnp.concatenate(
                        [jnp.where(first_blk, tab_ref[h, idx[g][i][2 * a]], tab_ref[h, idx[g][i][2 * a + 1]])
                         for a in range(NA_KR // 2)] + [jnp.zeros((GRID_W, LC), F32)], axis=1))
            scores[g, s] = scores[g, s] + jnp.concatenate(rows, axis=0)
        for g, s in chains:
            vv = jnp.concatenate([kv_ref[pl.ds(base[g], n_loc), NA_W + s * LANE:NA_W + (s + 1) * LANE],
                                  kv_ref[0:LC, NA_W + s * LANE:NA_W + (s + 1) * LANE]], axis=0)
            finish(g, s, scores[g, s], vv)

    @pl.when(j < n_ctx_steps)
    def _():
        scores = {(g, s): _dot_nt(stacked_q(g, s), kv_ref[0:LC, s * LANE:(s + 1) * LANE]) for g, s in chains}
        for g, s in chains:
            finish(g, s, scores[g, s], kv_ref[0:LC, NA_W + s * LANE:NA_W + (s + 1) * LANE])


def _natten(p, kv16, bias_tab, layer):
    bsz, t, _ = p.shape
    nq = t // NA_BQ
    return pl.pallas_call(
        _na_kernel,
        grid=(bsz, nq),
        in_specs=[pl.BlockSpec((None, NA_BQ, NA_W), lambda b, j: (b, j, C_NQ // NA_W)),
                  pl.BlockSpec((None, t, 2 * NA_W), lambda b, j: (b, 0, 0)),
                  pl.BlockSpec((None, NA_BQ, NA_W), lambda b, j: (b, j, C_ZN // NA_W)),
                  pl.BlockSpec((None, NA_HEADS, 2 * NA_ROWS, GRID_W, LANE), lambda b, j: (layer, 0, 0, 0, 0))],
        out_specs=pl.BlockSpec((None, NA_BQ, NA_W), lambda b, j: (b, j, 0)),
        out_shape=jax.ShapeDtypeStruct((bsz, t, NA_W), F32),
        compiler_params=pltpu.CompilerParams(vmem_limit_bytes=VMEM_BIG),
        name="natten",
    )(p, kv16, p, bias_tab)


def _swap16(x):
    lane = lax.broadcasted_iota(jnp.int32, x.shape, 1)
    return jnp.where(lane % 32 < 16, pltpu.roll(x, LANE - 16, 1), pltpu.roll(x, 16, 1))


def _head_sum_sq(x, hsum):
    return _dot_split(x * x, hsum)


def _gdn_prep_kernel(q_ref, k_ref, v_ref, cw_ref, cos_ref, sin_ref, hsum_ref, qo_ref, ko_ref, vo_ref,
                     qpad_ref, kpad_ref, vpad_ref):
    t = q_ref.shape[0]
    hsum = hsum_ref[...]
    streams = ((q_ref, qpad_ref, qo_ref, "q"), (k_ref, kpad_ref, ko_ref, "k"), (v_ref, vpad_ref, vo_ref, "v"))
    for src_ref, pad_ref, _, _ in streams:
        _fill_padded(src_ref, pad_ref, t)

    def body(c, carry):
        t0 = pl.multiple_of(c * RC, RC)
        xs = [_silu(_conv_chunk(pad_ref, cw_ref[n], t0)) for n, (_, pad_ref, _, _) in enumerate(streams)]
        ssq = [_head_sum_sq(x, hsum) for x in xs[:2]]
        for n, (_, _, dst_ref, mode) in enumerate(streams):
            x = xs[n]
            if mode != "v":
                x = x * lax.rsqrt(ssq[n] + NORM_EPS)
                x = x * cos_ref[pl.ds(t0, RC), :] + _swap16(x) * sin_ref[pl.ds(t0, RC), :]
            if mode == "q":
                x = x * (HD ** -0.5)
            dst_ref[pl.ds(t0, RC), :] = x
        return carry

    _chunk_loop(t // RC, body, 8)


def _gdn_prep(p, conv_w3, cos_t, sin_t, hsum, layer):
    bsz, t, _ = p.shape
    ns = GDN_W // LANE
    col = lambda c0: (lambda b, s: (b, 0, c0 // LANE + s))
    out = jax.ShapeDtypeStruct((bsz, t, GDN_W), F32)
    return pl.pallas_call(
        _gdn_prep_kernel,
        grid=(bsz, ns),
        in_specs=[pl.BlockSpec((None, t, LANE), col(C_GQ)),
                  pl.BlockSpec((None, t, LANE), col(C_GK)),
                  pl.BlockSpec((None, t, LANE), col(C_GV)),
                  pl.BlockSpec((None, 3, CONV_K, LANE), lambda b, s: (layer, 0, 0, s)),
                  pl.BlockSpec((t, LANE), lambda b, s: (0, 0)),
                  pl.BlockSpec((t, LANE), lambda b, s: (0, 0)),
                  pl.BlockSpec((LANE, LANE), lambda b, s: (0, 0))],
        out_specs=[pl.BlockSpec((None, t, LANE), lambda b, s: (b, 0, s))] * 3,
        out_shape=[out, out, out],
        scratch_shapes=[pltpu.VMEM((t + PAD_ROWS, LANE), F32)] * 3,
        compiler_params=pltpu.CompilerParams(vmem_limit_bytes=VMEM_BIG),
        name="gdn_prep",
    )(p, p, p, conv_w3, cos_t, sin_t, hsum)


GB = 256
NCH = GB // CHUNK
GDN_NB = 4


def _gdn_scan_kernel(qf_ref, kf_ref, vf_ref, gf_ref, qb_ref, kb_ref, vb_ref, gb_ref,
                     esel_ref, gvec_ref, of_ref, ob_ref, s_ref):
    step = pl.program_id(1)
    w4 = GDN_W
    nb = qf_ref.shape[0]

    @pl.when(step == 0)
    def _():
        s_ref[...] = jnp.zeros_like(s_ref)

    row = lax.broadcasted_iota(jnp.int32, (CHUNK, w4), 0)
    jl = lax.broadcasted_iota(jnp.int32, (CHUNK, w4), 1) % HD
    r2 = lax.broadcasted_iota(jnp.int32, (w4, w4), 0)
    c2 = lax.broadcasted_iota(jnp.int32, (w4, w4), 1)
    bd = (r2 // HD) == (c2 // HD)
    eye = (jl == row).astype(F32)
    blk16 = (row // 16) == (jl // 16)
    blk32 = (row // 32) == (jl // 32)
    alog = gvec_ref[0:1, :]
    dtb = gvec_ref[1:2, :]
    glane = lax.broadcasted_iota(jnp.int32, (GB, LANE), 1)

    def expand(y):
        yb = y.astype(BF16)
        return jnp.where(bd, jnp.concatenate([yb, yb, yb, yb], axis=0), jnp.zeros((), BF16))

    def mm(x, y):
        return _dot(x.astype(BF16), expand(y))

    refs = ((qf_ref, kf_ref, vf_ref, gf_ref), (qb_ref, kb_ref, vb_ref, gb_ref))
    chains = []
    for n, d in [(n, d) for n in range(nb) for d in range(2)]:
        q_ref, k_ref, v_ref, g_ref = (r.at[n] for r in refs[d])
        raw = g_ref[...]
        comp = jnp.where(glane < 2 * GDN_HEADS, _sigmoid(raw), -jnp.exp(alog) * _softplus(raw + dtb))
        gexp = _dot_split(comp, esel_ref[d])
        if d == 0:
            incl, strict, upto, tri = jl <= row, jl < row, row <= jl, bd & (c2 <= r2)
        else:
            incl, strict, upto, tri = jl >= row, jl > row, row >= jl, bd & (c2 >= r2)
        last = CHUNK - 1 if d == 0 else 0
        g_all = gexp[:, w4:2 * w4]
        g_hi, g_lo = _split2(g_all)
        tri = tri.astype(BF16)
        gcum_all = _dot(tri, g_hi) + _dot(tri, g_lo)
        for c in range(NCH):
            rows = slice(c * CHUNK, (c + 1) * CHUNK)
            ch = dict(bi=n, d=d, c=c, q=q_ref[rows, :], k=k_ref[rows, :], v=v_ref[rows, :], beta=gexp[rows, 0:w4],
                      gcum=gcum_all[rows], incl=incl, strict=strict, last=last)
            grow = jnp.sum(jnp.where(upto, g_all[rows], 0.0), axis=0, keepdims=True)
            ch["decay"] = jnp.where(incl, jnp.exp(jnp.minimum(ch["gcum"] - grow, 0.0)), 0.0)
            ch["kbeta"] = ch["k"] * ch["beta"]
            chains.append(ch)

    for ch in chains:
        prod = _dot_nt(jnp.concatenate([ch["kbeta"], ch["q"]], axis=0).astype(BF16), expand(ch["k"]))
        ch["n"] = jnp.where(ch["strict"], prod[0:CHUNK] * ch["decay"], 0.0)
        ch["qk"] = jnp.where(ch["incl"], prod[CHUNK:2 * CHUNK] * ch["decay"], 0.0)
        n16 = jnp.where(blk16, ch["n"], 0.0)
        ch["t"] = eye - n16
        ch["pw"] = n16
    for ch in chains:
        ch["pw"] = mm(ch["pw"], ch["pw"])
    for _ in range(2):
        for ch in chains:
            both = mm(jnp.concatenate([ch["t"], ch["pw"]], axis=0), ch["pw"])
            ch["t"] = ch["t"] + both[0:CHUNK]
            ch["pw"] = both[CHUNK:2 * CHUNK]
    for ch in chains:
        ch["t"] = ch["t"] + mm(ch["t"], ch["pw"])
    for off in (jnp.where(blk32 & ~blk16, 1.0, 0.0), jnp.where(blk32, 0.0, 1.0)):
        for ch in chains:
            ch["te"] = mm(ch["t"], ch["n"] * off)
        for ch in chains:
            ch["t"] = ch["t"] - mm(ch["te"], ch["t"])
    for ch in chains:
        eg = jnp.exp(ch["gcum"])
        ch["u"] = mm(ch["t"], ch["v"] * ch["beta"])
        w = mm(ch["t"], ch["kbeta"] * eg)
        glast = ch["gcum"][ch["last"]:ch["last"] + 1, :]
        ch["wq"] = jnp.concatenate([w, ch["q"] * eg], axis=0).astype(BF16)
        ch["kg"] = (ch["k"] * jnp.exp(glast - ch["gcum"])).astype(BF16)
        ch["eglast"] = jnp.exp(glast)

    by_seq = {(ch["bi"], ch["d"], ch["c"]): ch for ch in chains}
    seqs = [(n, d) for n in range(nb) for d in range(2)]
    state = {sq: s_ref[2 * sq[0] + sq[1]] for sq in seqs}
    res = {}
    for i in range(NCH):
        cur = {sq: by_seq[sq[0], sq[1], i if sq[1] == 0 else NCH - 1 - i] for sq in seqs}
        ws = {sq: _dot(cur[sq]["wq"], state[sq].astype(BF16)) for sq in seqs}
        v_new = {sq: cur[sq]["u"] - ws[sq][0:CHUNK] for sq in seqs}
        for sq in seqs:
            res[sq + (cur[sq]["c"],)] = ws[sq][CHUNK:2 * CHUNK] + mm(cur[sq]["qk"], v_new[sq])
        for sq in seqs:
            upd = _dot_tn(cur[sq]["kg"], v_new[sq].astype(BF16))
            state[sq] = state[sq] * cur[sq]["eglast"] + jnp.where(bd, upd, 0.0)
    outs = (of_ref, ob_ref)
    for n, d in seqs:
        s_ref[2 * n + d] = state[n, d]
        for c in range(NCH):
            outs[d][n, c * CHUNK:(c + 1) * CHUNK, :] = res[n, d, c]


def _gdn_scan(qn, kn, vn, p, esel, gvec, layer):
    bsz, t, _ = p.shape
    nsteps = t // GB
    fwd = lambda b, i: (b, i, 0)
    bwd = lambda b, i: (b, jnp.where(i == 0, 0, nsteps - i), 0)
    gcol = C_GATE // LANE
    fwd_g = lambda b, i: (b, i, gcol)
    bwd_g = lambda b, i: (b, jnp.where(i == 0, 0, nsteps - i), gcol)
    nb = math.gcd(bsz, GDN_NB)
    blk = lambda im: pl.BlockSpec((nb, GB, GDN_W), im)
    out = jax.ShapeDtypeStruct((bsz, t, GDN_W), F32)
    return pl.pallas_call(
        _gdn_scan_kernel,
        grid=(bsz // nb, nsteps),
        in_specs=[blk(fwd), blk(fwd), blk(fwd), pl.BlockSpec((nb, GB, LANE), fwd_g),
                  blk(bwd), blk(bwd), blk(bwd), pl.BlockSpec((nb, GB, LANE), bwd_g),
                  pl.BlockSpec((2, LANE, 2 * GDN_W), lambda b, i: (0, 0, 0)),
                  pl.BlockSpec((None, 2, LANE), lambda b, i: (layer, 0, 0))],
        out_specs=[blk(fwd), blk(bwd)],
        out_shape=[out, out],
        scratch_shapes=[pltpu.VMEM((2 * nb, GDN_W, GDN_W), F32)],
        compiler_params=pltpu.CompilerParams(vmem_limit_bytes=VMEM_BIG),
        name="gdn_scan",
    )(qn, kn, vn, p, qn, kn, vn, p, esel, gvec)


def _outproj_kernel(*refs, n_tok, skip, fuse_next):
    (h_ref, nb_ref, of_ref, ob_ref, zg_ref, m_ref, w_ref, nw_ref, hsum_ref, lng_ref, lnb_ref) = refs[n_tok:n_tok + 11]
    o_ref = refs[n_tok + 13] if fuse_next else refs[n_tok + 11]
    o = of_ref[...] + ob_ref[...]
    ms = _head_sum_sq(o, hsum_ref[...]) * (1.0 / HD)
    og = o * lax.rsqrt(ms + NORM_EPS) * nw_ref[...] * _silu(zg_ref[...])
    mixed = jnp.concatenate([h_ref[...].astype(BF16), nb_ref[...].astype(BF16), og.astype(BF16)], axis=1)
    y = _dot(mixed, w_ref[...])
    gate = m_ref[...][:, 2 * D:3 * D]
    z = DEEPNORM_ALPHA * _load_tokens(refs[:n_tok], pl.program_id(1) + skip) + gate * y
    mu = jnp.mean(z, axis=-1, keepdims=True)
    zc = z - mu
    var = jnp.mean(zc * zc, axis=-1, keepdims=True)
    x_new = zc * lax.rsqrt(var + LN_EPS) * lng_ref[...] + lnb_ref[...]
    o_ref[...] = x_new
    if fuse_next:
        m2_ref, win_ref = refs[n_tok + 11:n_tok + 13]
        p_ref, kv_ref = refs[n_tok + 14:n_tok + 16]
        m2 = m2_ref[...]
        u = x_new * (1.0 + m2[:, D:2 * D]) + m2[:, :D]
        p = _dot(u.astype(BF16), win_ref[...])
        p_ref[:, :W_KV0] = p[:, :W_KV0]
        p_ref[:, W_KV0:] = p[:, W_KV1:]
        kv_ref[...] = p[:, W_KV0:W_KV1].astype(BF16)


def _out_proj(tokens, hg, nb, o_f, o_b, p, modsel, w_out_b, nw4, hsum4, ln_g, ln_b, layer, latent_only,
              w_in_p=None):
    bsz, t = _token_shape(tokens)
    skip = LC // TM if latent_only else 0
    nt = t // TM - skip
    tok_specs, tok_ops = _token_specs(tokens, skip)
    row = lambda width: pl.BlockSpec((None, TM, width), lambda b, i: (b, i + skip, 0))
    vec = lambda: pl.BlockSpec((None, 1, D), lambda b, i: (layer, 0, 0))
    fuse = w_in_p is not None
    out_specs = [pl.BlockSpec((None, TM, D), lambda b, i: (b, i, 0))]
    out_shape = [jax.ShapeDtypeStruct((bsz, nt * TM, D), F32)]
    extra_specs, extra_ops = [], []
    if fuse:
        assert not latent_only
        extra_specs = [pl.BlockSpec((None, None, 1, 3 * D), lambda b, i: (layer + 1, 2 * b + jnp.minimum(i, 1), 0, 0)),
                       pl.BlockSpec((None, D, D_INP), lambda b, i: (layer + 1, 0, 0))]
        extra_ops = [modsel, w_in_p]
        out_specs += [pl.BlockSpec((None, TM, D_F32), lambda b, i: (b, i, 0)),
                      pl.BlockSpec((None, TM, 2 * NA_W), lambda b, i: (b, i, 0))]
        out_shape += [jax.ShapeDtypeStruct((bsz, t, D_F32), F32), jax.ShapeDtypeStruct((bsz, t, 2 * NA_W), BF16)]
    res = pl.pallas_call(
        functools.partial(_outproj_kernel, n_tok=len(tok_ops), skip=skip, fuse_next=fuse),
        grid=(bsz, nt),
        in_specs=tok_specs + [
                  row(RG_W), row(NA_W), row(GDN_W), row(GDN_W),
                  pl.BlockSpec((None, TM, GDN_W), lambda b, i: (b, i + skip, C_ZG // GDN_W)),
                  pl.BlockSpec((None, None, 1, 3 * D),
                               lambda b, i: (layer, 2 * b + jnp.minimum(i + skip, 1), 0, 0)),
                  pl.BlockSpec((None, D, D), lambda b, i: (layer, 0, 0)),
                  pl.BlockSpec((None, 1, GDN_W), lambda b, i: (layer, 0, 0)),
                  pl.BlockSpec((GDN_W, GDN_W), lambda b, i: (0, 0)),
                  vec(), vec()] + extra_specs,
        out_specs=out_specs,
        out_shape=out_shape,
        compiler_params=pltpu.CompilerParams(vmem_limit_bytes=VMEM_BIG),
        name="out_in_proj" if fuse else "out_proj",
    )(*tok_ops, hg, nb, o_f, o_b, p, modsel, w_out_b, nw4, hsum4, ln_g, ln_b, *extra_ops)
    return res if fuse else res[0]


def _na_bias_table(na_rpb):
    jq = np.arange(GRID_W)
    col_start = np.clip(jq - NA_COLS // 2, 0, GRID_W - NA_COLS)
    kc = np.arange(GRID_W)
    inside = (kc[None, :] >= col_start[:, None]) & (kc[None, :] < col_start[:, None] + NA_COLS)
    col_off = kc[None, :] - jq[:, None] + NA_COLS - 1
    onehot = (np.arange(2 * NA_COLS - 1)[:, None, None] == col_off[None]).astype(np.float32)
    toeplitz = jnp.einsum('lhrx,xjc->lhrjc', na_rpb, jnp.asarray(onehot), precision=HIGHEST)
    tab = jnp.where(inside[None, None, None], toeplitz, NEG)
    tab = jnp.concatenate([tab, jnp.full((DEPTH, NA_HEADS, 1, GRID_W, GRID_W), NEG, F32)], axis=2)
    return jnp.concatenate([tab, tab], axis=-1)


def _rope_tables(seq):
    pos = np.arange(seq)
    rows_pos, cols_pos = pos // GRID_W, pos % GRID_W
    half = HD // 2
    nf = half // 2
    inv_freq = (np.float32(ROPE_BASE) ** (-np.arange(nf, dtype=np.float32) / np.float32(nf))).astype(np.float32)
    lane = np.arange(LANE)
    jl = lane % HD
    use_row = jl < half
    f = (jl % half) % nf
    sign = np.where((jl % half) < nf, -1.0, 1.0)
    pos_l = np.where(use_row[None, :], rows_pos[:, None], cols_pos[:, None]).astype(np.float32)
    ang = (pos_l * inv_freq[f][None, :]).astype(np.float64)
    cos_t = np.concatenate([np.ones((LC, LANE)), np.cos(ang)], axis=0).astype(np.float32)
    sin_t = np.concatenate([np.zeros((LC, LANE)), np.sin(ang) * sign[None, :]], axis=0).astype(np.float32)
    return jnp.asarray(cos_t), jnp.asarray(sin_t)


def _gate_select():
    e = np.zeros((2, LANE, 2 * GDN_W), np.float32)
    for d in range(2):
        for l in range(GDN_W):
            h = l // HD
            e[d, d * GDN_HEADS + h, l] = 1.0
            e[d, 2 * GDN_HEADS + d * GDN_HEADS + h, GDN_W + l] = 1.0
    return jnp.asarray(e).astype(BF16)


def _head_sum(width):
    i = np.arange(width)
    return jnp.asarray((i[:, None] // HD == i[None, :] // HD).astype(np.float32)).astype(BF16)


def kernel(x, c, ctx, c_ctx, w_mod, b_mod, w_in, conv_w, rg_wa, rg_ba, rg_wx, rg_bx, rg_lam, na_rpb, gdn_alog,
           gdn_dtb, gdn_nw, w_out, ln_g, ln_b):
    bsz, seq, _ = x.shape
    assert ctx.shape[1] == LC and seq == GRID_W * GRID_W

    c8 = jnp.concatenate([c, c_ctx[None], jnp.zeros((SUB - bsz - 1, D), F32)], axis=0)
    w_in_p = jnp.pad(w_in.astype(BF16), ((0, 0), (0, 0), (0, D_INP - D_IN)))
    w_out_b = w_out.astype(BF16)
    ns = RG_W // LANE
    eye2 = jnp.eye(2, dtype=F32)

    def block_diag(wt):
        wt = wt.reshape(DEPTH, 2, ns, 2, RG_BLOCK, RG_BLOCK)
        return jnp.einsum('ldsjae,jk->ldsjake', wt, eye2).reshape(DEPTH, 2, ns, LANE, LANE)

    wa, wx = block_diag(rg_wa), block_diag(rg_wx)
    rg_w = jnp.concatenate([wa[:, 0], wx[:, 0], wa[:, 1], wx[:, 1]], axis=-1).astype(BF16)
    slab = lambda v: v.reshape(DEPTH, ns, 1, LANE)
    rg_b = jnp.concatenate([slab(rg_ba[:, 0]), slab(rg_bx[:, 0]), slab(rg_ba[:, 1]), slab(rg_bx[:, 1])], axis=-1)
    rg_lam2 = jnp.concatenate([slab(rg_lam[:, 0]), slab(rg_lam[:, 1])], axis=-1)
    conv_w3 = jnp.stack([conv_w[:, :, C_GQ:C_GK], conv_w[:, :, C_GK:C_GV], conv_w[:, :, C_GV:C_ZA]], axis=1)
    bias_tab = _na_bias_table(na_rpb)
    cos_t, sin_t = _rope_tables(seq)
    esel = _gate_select()
    pad = jnp.zeros((DEPTH, LANE - 4 * GDN_HEADS), F32)
    zero8 = jnp.zeros((DEPTH, 2 * GDN_HEADS), F32)
    gvec = jnp.stack([jnp.concatenate([zero8, gdn_alog.reshape(DEPTH, -1), pad], axis=-1),
                      jnp.concatenate([zero8, gdn_dtb.reshape(DEPTH, -1), pad], axis=-1)], axis=1)
    nw4 = jnp.tile(gdn_nw, (1, GDN_HEADS)).reshape(DEPTH, 1, GDN_W)
    hsum2, hsum4 = _head_sum(LANE), _head_sum(GDN_W)
    ln_g3, ln_b3 = ln_g.reshape(DEPTH, 1, D), ln_b.reshape(DEPTH, 1, D)

    mods = _modulation(c8, w_mod, b_mod)
    ctx_rows = jnp.broadcast_to(mods[:, bsz:bsz + 1], (DEPTH, bsz, 3 * D))
    modsel = jnp.stack([ctx_rows, mods[:, :bsz]], axis=2).reshape(DEPTH, 2 * bsz, 1, 3 * D)

    xa = (ctx, x)
    p, kv16 = _in_proj(xa, modsel, w_in_p, 0)
    for layer in range(DEPTH):
        hg = _rglru(p, conv_w, rg_w, rg_b, rg_lam2, layer)
        nb = _natten(p, kv16, bias_tab, layer)
        qn, kn, vn = _gdn_prep(p, conv_w3, cos_t, sin_t, hsum2, layer)
        o_f, o_b = _gdn_scan(qn, kn, vn, p, esel, gvec, layer)
        if layer < DEPTH - 1:
            xa, p, kv16 = _out_proj(xa, hg, nb, o_f, o_b, p, modsel, w_out_b, nw4, hsum4, ln_g3, ln_b3, layer,
                                    latent_only=False, w_in_p=w_in_p)
        else:
            xa = _out_proj(xa, hg, nb, o_f, o_b, p, modsel, w_out_b, nw4, hsum4, ln_g3, ln_b3, layer,
                           latent_only=True)
    return xa
```
